```python
import jax, jax.numpy as jnp
from jax import lax
import numpy as np

D_MODEL = 1024
BATCH = 8
SEQ = 2048
DEPTH = 2

CHUNK = 64
EPS = 1e-6
LRU_WIDTH = D_MODEL
LRU_BLOCKS = 16
LRU_BLOCK_DIM = LRU_WIDTH // LRU_BLOCKS
LRU_CONV = 4
LRU_C = 8.0
RET_QK_DIM = 256
RET_V_DIM = 512
RET_HEADS = D_MODEL // RET_QK_DIM
RET_QK = RET_HEADS * RET_QK_DIM
RET_V = RET_HEADS * RET_V_DIM
ROPE_BASE = 10000.0
SC_WIDTH = D_MODEL
SC_CONV = 3
D_FF = 2816
SPLIT_SIZES = (LRU_WIDTH, LRU_WIDTH, RET_QK, RET_QK, RET_V, RET_V,
               SC_WIDTH, SC_WIDTH, SC_WIDTH, 3 * D_MODEL)
MIX_IN = sum(SPLIT_SIZES)

kernel_name = "hybrid_lru_retention_shortconv_macaron"


def rmsnorm(x, g):
    xf = x.astype(jnp.float32)
    xf = xf * lax.rsqrt(jnp.mean(xf * xf, axis=-1, keepdims=True) + EPS)
    return xf.astype(x.dtype) * g


def swiglu(u, w_in, w_out):
    gate, up = jnp.split(u @ w_in, 2, axis=-1)
    return (jax.nn.silu(gate) * up) @ w_out


def causal_depthwise_conv(x, w):
    K, C = w.shape
    return lax.conv_general_dilated(
        x, w[:, None, :].astype(x.dtype), window_strides=(1,), padding=[(K - 1, 0)],
        dimension_numbers=("NWC", "WIO", "NWC"), feature_group_count=C)


def _lin_rec_combine(e1, e2):
    a1, b1 = e1
    a2, b2 = e2
    return a1 * a2, a2 * b1 + b2


def rg_lru(x, w_a, b_a, w_x, b_x, lam):
    Bsz, S, W = x.shape
    xf = x.astype(jnp.float32)
    xb = xf.reshape(Bsz, S, LRU_BLOCKS, LRU_BLOCK_DIM)
    r = jax.nn.sigmoid(jnp.einsum("bsgi,gij->bsgj", xb, w_a.astype(jnp.float32)).reshape(Bsz, S, W)
                       + b_a.astype(jnp.float32))
    i = jax.nn.sigmoid(jnp.einsum("bsgi,gij->bsgj", xb, w_x.astype(jnp.float32)).reshape(Bsz, S, W)
                       + b_x.astype(jnp.float32))
    log_a = -LRU_C * r * jax.nn.softplus(-lam.astype(jnp.float32))
    a = jnp.exp(log_a)
    u = jnp.sqrt(-jnp.expm1(2.0 * log_a)) * (i * xf)
    _, h = lax.associative_scan(_lin_rec_combine, (a, u), axis=1)
    return h.astype(x.dtype)


def rope(t, positions):
    half = t.shape[-1] // 2
    inv_freq = jnp.power(ROPE_BASE, -jnp.arange(half, dtype=jnp.float32) / half)
    ang = positions.astype(jnp.float32)[:, None] * inv_freq[None, :]
    cos = jnp.cos(ang)[None, :, None, :]
    sin = jnp.sin(ang)[None, :, None, :]
    t1, t2 = t[..., :half], t[..., half:]
    return jnp.concatenate([t1 * cos - t2 * sin, t1 * sin + t2 * cos], axis=-1)


def retention(q, k, v, positions):
    Bsz, S, H, dk = q.shape
    dv = v.shape[-1]
    N = S // CHUNK
    q = rope(q.astype(jnp.float32), positions)
    k = rope(k.astype(jnp.float32), positions) * (dk ** -0.5)
    v = v.astype(jnp.float32)
    log_g = jnp.log1p(-jnp.power(2.0, -5.0 - jnp.arange(H, dtype=jnp.float32)))
    idx = jnp.arange(CHUNK, dtype=jnp.float32)
    intra_decay = jnp.exp(log_g[:, None, None] * jnp.abs(idx[:, None] - idx[None, :]))
    q_decay = jnp.exp(log_g[:, None] * (idx[None, :] + 1.0))
    k_decay = jnp.exp(log_g[:, None] * (CHUNK - 1.0 - idx[None, :]))
    chunk_decay = jnp.exp(log_g * CHUNK)

    def to_chunks(t):
        return t.reshape(Bsz, N, CHUNK, H, t.shape[-1]).transpose(1, 0, 3, 2, 4)

    qc, kc, vc = to_chunks(q), to_chunks(k), to_chunks(v)
    scores = jnp.einsum("nbhid,nbhjd->nbhij", qc, kc) * intra_decay
    intra = jnp.einsum("nbhij,nbhje->nbhie", scores, vc)

    def step(state, inp):
        qn, kn, vn = inp
        cross = jnp.einsum("bhid,bhde->bhie", qn, state) * q_decay[None, :, :, None]
        state = state * chunk_decay[None, :, None, None] + jnp.einsum(
            "bhjd,bhje->bhde", kn * k_decay[None, :, :, None], vn)
        return state, cross

    state0 = jnp.zeros((Bsz, H, dk, dv), jnp.float32)
    _, cross = lax.scan(step, state0, (qc, kc, vc))
    out = (intra + cross).transpose(1, 0, 3, 2, 4).reshape(Bsz, S, H, dv)
    out = out * lax.rsqrt(jnp.mean(out * out, axis=-1, keepdims=True) + EPS)
    return out


def hybrid_mixer(u, positions, w_in, lru_conv_w, lru_conv_b, lru_w_a, lru_b_a, lru_w_x, lru_b_x,
                 lru_lambda, w_lru_out, w_ret_out, sc_conv_w, w_sc_out, w_out):
    Bsz, S, _ = u.shape
    proj = u @ w_in
    cuts = [int(c) for c in np.cumsum(SPLIT_SIZES)[:-1]]
    xa, ya, q, k, v, g, sb, sc, sh, gates = jnp.split(proj, cuts, axis=-1)
    xa = causal_depthwise_conv(xa, lru_conv_w) + lru_conv_b
    ha = rg_lru(xa, lru_w_a, lru_b_a, lru_w_x, lru_b_x, lru_lambda)
    oa = (ha * jax.nn.gelu(ya)) @ w_lru_out
    ret = retention(q.reshape(Bsz, S, RET_HEADS, RET_QK_DIM),
                    k.reshape(Bsz, S, RET_HEADS, RET_QK_DIM),
                    v.reshape(Bsz, S, RET_HEADS, RET_V_DIM), positions)
    ob = (jax.nn.silu(g) * ret.reshape(Bsz, S, RET_V).astype(u.dtype)) @ w_ret_out
    oc = (sb * causal_depthwise_conv(sc * sh, sc_conv_w)) @ w_sc_out
    ga, gb, gc = jnp.split(jax.nn.sigmoid(gates), 3, axis=-1)
    return (ga * oa + gb * ob + gc * oc) @ w_out


def setup_inputs(seed: int = 0) -> dict:
    key = jax.random.key(seed)
    ks = jax.random.split(key, 32)
    L = DEPTH

    def nrm(k, shape, scale):
        return jax.random.normal(k, shape, jnp.float32) * scale

    def gain(k):
        return 1.0 + 0.02 * jax.random.normal(k, (L, D_MODEL), jnp.float32)

    a_target = jax.random.uniform(ks[2], (L, LRU_WIDTH), jnp.float32, minval=0.9, maxval=0.999)
    s = a_target ** (1.0 / LRU_C)
    lru_lambda = jnp.log(s) - jnp.log1p(-s)
    offset = CHUNK * jax.random.randint(ks[3], (), 0, 64, dtype=jnp.int32)
    return {
        "x": nrm(ks[0], (BATCH, SEQ, D_MODEL), 1.0),
        "positions": offset + jnp.arange(SEQ, dtype=jnp.int32),
        "ffn1_pre_g": gain(ks[4]),
        "ffn1_w_in": nrm(ks[5], (L, D_MODEL, 2 * D_FF), D_MODEL ** -0.5),
        "ffn1_w_out": nrm(ks[6], (L, D_FF, D_MODEL), D_FF ** -0.5),
        "ffn1_post_g": gain(ks[7]),
        "mix_pre_g": gain(ks[8]),
        "w_mix_in": nrm(ks[9], (L, D_MODEL, MIX_IN), D_MODEL ** -0.5),
        "lru_conv_w": nrm(ks[10], (L, LRU_CONV, LRU_WIDTH), LRU_CONV ** -0.5),
        "lru_conv_b": nrm(ks[11], (L, LRU_WIDTH), 0.01),
        "lru_w_a": nrm(ks[12], (L, LRU_BLOCKS, LRU_BLOCK_DIM, LRU_BLOCK_DIM), LRU_BLOCK_DIM ** -0.5),
        "lru_b_a": nrm(ks[13], (L, LRU_WIDTH), 0.01),
        "lru_w_x": nrm(ks[14], (L, LRU_BLOCKS, LRU_BLOCK_DIM, LRU_BLOCK_DIM), LRU_BLOCK_DIM ** -0.5),
        "lru_b_x": nrm(ks[15], (L, LRU_WIDTH), 0.01),
        "lru_lambda": lru_lambda,
        "w_lru_out": nrm(ks[16], (L, LRU_WIDTH, D_MODEL), LRU_WIDTH ** -0.5),
        "w_ret_out": nrm(ks[17], (L, RET_V, D_MODEL), RET_V ** -0.5),
        "sc_conv_w": nrm(ks[18], (L, SC_CONV, SC_WIDTH), SC_CONV ** -0.5),
        "w_sc_out": nrm(ks[19], (L, SC_WIDTH, D_MODEL), SC_WIDTH ** -0.5),
        "w_mix_out": nrm(ks[20], (L, D_MODEL, D_MODEL), D_MODEL ** -0.5),
        "mix_post_g": gain(ks[21]),
        "ffn2_pre_g": gain(ks[22]),
        "ffn2_w_in": nrm(ks[23], (L, D_MODEL, 2 * D_FF), D_MODEL ** -0.5),
        "ffn2_w_out": nrm(ks[24], (L, D_FF, D_MODEL), D_FF ** -0.5),
        "ffn2_post_g": gain(ks[25]),
    }


def reference(x, positions, ffn1_pre_g, ffn1_w_in, ffn1_w_out, ffn1_post_g, mix_pre_g, w_mix_in,
              lru_conv_w, lru_conv_b, lru_w_a, lru_b_a, lru_w_x, lru_b_x, lru_lambda, w_lru_out,
              w_ret_out, sc_conv_w, w_sc_out, w_mix_out, mix_post_g, ffn2_pre_g, ffn2_w_in,
              ffn2_w_out, ffn2_post_g):
    for l in range(DEPTH):
        h = swiglu(rmsnorm(x, ffn1_pre_g[l]), ffn1_w_in[l], ffn1_w_out[l])
        x = x + 0.5 * rmsnorm(h, ffn1_post_g[l])
        h = hybrid_mixer(rmsnorm(x, mix_pre_g[l]), positions, w_mix_in[l], lru_conv_w[l],
                         lru_conv_b[l], lru_w_a[l], lru_b_a[l], lru_w_x[l], lru_b_x[l],
                         lru_lambda[l], w_lru_out[l], w_ret_out[l], sc_conv_w[l], w_sc_out[l],
                         w_mix_out[l])
        x = x + rmsnorm(h, mix_post_g[l])
        h = swiglu(rmsnorm(x, ffn2_pre_g[l]), ffn2_w_in[l], ffn2_w_out[l])
        x = x + 0.5 * rmsnorm(h, ffn2_post_g[l])
    return x
```

```python
import functools

import numpy as np
import jax
import jax.numpy as jnp
from jax import lax
from jax.experimental import pallas as pl
from jax.experimental.pallas import tpu as pltpu

D_MODEL = 1024
BATCH = 8
SEQ = 2048
DEPTH = 2
CHUNK = 64
EPS = 1e-6
LRU_WIDTH = D_MODEL
LRU_BLOCKS = 16
LRU_BLOCK_DIM = LRU_WIDTH // LRU_BLOCKS
LRU_CONV = 4
LRU_C = 8.0
RET_QK_DIM = 256
RET_V_DIM = 512
RET_HEADS = D_MODEL // RET_QK_DIM
RET_QK = RET_HEADS * RET_QK_DIM
RET_V = RET_HEADS * RET_V_DIM
ROPE_BASE = 10000.0
SC_WIDTH = D_MODEL
SC_CONV = 3
D_FF = 2816
MIX_IN = 2 * LRU_WIDTH + 2 * RET_QK + 2 * RET_V + 3 * SC_WIDTH + 3 * D_MODEL
TOKENS = BATCH * SEQ

V7X_MXU_DIM = 256
V7X_SUBLANES = 8
V7X_VMEM_LIMIT_BYTES = 56 * 1024 * 1024

COL_XA, COL_YA, COL_Q, COL_K = 0, 1, 2, 3
COL_V, COL_G = 2, 3
COL_SB, COL_SC, COL_SH, COL_GATES = 8, 9, 10, 11

FFN_TM = 512
FFN_TF = 256
PROJ_TM = 1024
PROJ_TN = 2048
LRU_TT = 256
RET_L = 256
OUT_TT = 512

BF16 = jnp.bfloat16
F32 = jnp.float32


def _params(semantics):
    return pltpu.CompilerParams(dimension_semantics=semantics,
                                vmem_limit_bytes=V7X_VMEM_LIMIT_BYTES)


def _rms(x, g):
    return x * lax.rsqrt(jnp.mean(x * x, axis=-1, keepdims=True) + EPS) * g


def _ffn_kernel(x_ref, gpre_ref, wg_ref, wu_ref, wo_ref, gpost_ref, o_ref, xn_ref, acc_ref):
    j = pl.program_id(1)

    @pl.when(j == 0)
    def _():
        xn_ref[...] = _rms(x_ref[...], gpre_ref[...]).astype(BF16)
        acc_ref[...] = jnp.zeros_like(acc_ref)

    xn = xn_ref[...]
    gate = jnp.dot(xn, wg_ref[...], preferred_element_type=F32)
    up = jnp.dot(xn, wu_ref[...], preferred_element_type=F32)
    act = (jax.nn.silu(gate) * up).astype(BF16)
    acc_ref[...] += jnp.dot(act, wo_ref[...], preferred_element_type=F32)

    @pl.when(j == pl.num_programs(1) - 1)
    def _():
        o_ref[...] = x_ref[...] + 0.5 * _rms(acc_ref[...], gpost_ref[...])


def _ffn(x, layer, gpre, w_in, w_out, gpost):
    nf = D_FF // FFN_TF
    return pl.pallas_call(
        _ffn_kernel,
        grid=(TOKENS // FFN_TM, nf),
        in_specs=[
            pl.BlockSpec((FFN_TM, D_MODEL), lambda i, j: (i, 0)),
            pl.BlockSpec((None, 1, D_MODEL), lambda i, j: (layer, 0, 0)),
            pl.BlockSpec((None, D_MODEL, FFN_TF), lambda i, j: (layer, 0, j)),
            pl.BlockSpec((None, D_MODEL, FFN_TF), lambda i, j: (layer, 0, nf + j)),
            pl.BlockSpec((None, FFN_TF, D_MODEL), lambda i, j: (layer, j, 0)),
            pl.BlockSpec((None, 1, D_MODEL), lambda i, j: (layer, 0, 0)),
        ],
        out_specs=pl.BlockSpec((FFN_TM, D_MODEL), lambda i, j: (i, 0)),
        out_shape=jax.ShapeDtypeStruct((TOKENS, D_MODEL), F32),
        scratch_shapes=[pltpu.VMEM((FFN_TM, D_MODEL), BF16), pltpu.VMEM((FFN_TM, D_MODEL), F32)],
        compiler_params=_params(("parallel", "arbitrary")),
        name="ffn",
    )(x, gpre, w_in, w_in, w_out, gpost)


def _proj_kernel(x_ref, g_ref, w_ref, o_ref, xn_ref):
    @pl.when(pl.program_id(1) == 0)
    def _():
        xn_ref[...] = _rms(x_ref[...], g_ref[...]).astype(BF16)

    o_ref[...] = jnp.dot(xn_ref[...], w_ref[...], preferred_element_type=F32)


def _proj(x, layer, g, w):
    return pl.pallas_call(
        _proj_kernel,
        grid=(TOKENS // PROJ_TM, MIX_IN // PROJ_TN),
        in_specs=[
            pl.BlockSpec((PROJ_TM, D_MODEL), lambda i, j: (i, 0)),
            pl.BlockSpec((None, 1, D_MODEL), lambda i, j: (layer, 0, 0)),
            pl.BlockSpec((None, D_MODEL, PROJ_TN), lambda i, j: (layer, 0, j)),
        ],
        out_specs=pl.BlockSpec((PROJ_TM, PROJ_TN), lambda i, j: (i, j)),
        out_shape=jax.ShapeDtypeStruct((TOKENS, MIX_IN), F32),
        scratch_shapes=[pltpu.VMEM((PROJ_TM, D_MODEL), BF16)],
        compiler_params=_params(("parallel", "arbitrary")),
        name="mix_in",
    )(x, g, w)


def _causal_taps(ext_ref, cur, w, rows):
    taps = w.shape[0]
    ext_ref[V7X_SUBLANES:V7X_SUBLANES + rows, :] = cur
    acc = cur * w[taps - 1:taps]
    for d in range(1, taps):
        acc = acc + ext_ref[V7X_SUBLANES - d:V7X_SUBLANES - d + rows, :] * w[taps - 1 - d:taps - d]
    ext_ref[0:V7X_SUBLANES, :] = cur[rows - V7X_SUBLANES:rows]
    return acc


def _lru_kernel(xa_ref, ya_ref, cw_ref, cb_ref, wa_ref, ba_ref, wx_ref, bx_ref, lam_ref,
                o_ref, ext_ref, h_ref):
    rows = xa_ref.shape[0]

    @pl.when(pl.program_id(1) == 0)
    def _():
        ext_ref[0:V7X_SUBLANES, :] = jnp.zeros((V7X_SUBLANES, LRU_WIDTH), F32)
        h_ref[...] = jnp.zeros_like(h_ref)

    xc = _causal_taps(ext_ref, xa_ref[...], cw_ref[...], rows) + cb_ref[...]
    xcb = xc.astype(BF16)
    groups = LRU_WIDTH // V7X_MXU_DIM

    def gate(w_ref, b_ref):
        parts = [jnp.dot(xcb[:, g * V7X_MXU_DIM:(g + 1) * V7X_MXU_DIM], w_ref[g],
                         preferred_element_type=F32) for g in range(groups)]
        return jax.nn.sigmoid(jnp.concatenate(parts, axis=-1) + b_ref[...])

    r = gate(wa_ref, ba_ref)
    i = gate(wx_ref, bx_ref)
    nlam = -lam_ref[...]
    softplus = jnp.maximum(nlam, 0.0) + jnp.log1p(jnp.exp(-jnp.abs(nlam)))
    log_a = r * (-LRU_C * softplus)
    a = jnp.exp(log_a)
    u = jnp.sqrt(-jnp.tanh(log_a) * (a * a + 1.0)) * (i * xc)

    rowmod = lax.broadcasted_iota(jnp.int32, a.shape, 0) & (V7X_SUBLANES - 1)
    d = 1
    while d < V7X_SUBLANES:
        keep = rowmod >= d
        a_prev = jnp.where(keep, pltpu.roll(a, d, axis=0), 1.0)
        u_prev = jnp.where(keep, pltpu.roll(u, d, axis=0), 0.0)
        u = a * u_prev + u
        a = a * a_prev
        d *= 2

    h = h_ref[...]
    outs = []
    for g in range(rows // V7X_SUBLANES):
        lo = g * V7X_SUBLANES
        outs.append(a[lo:lo + V7X_SUBLANES] * h + u[lo:lo + V7X_SUBLANES])
        last = lo + V7X_SUBLANES - 1
        h = a[last:last + 1] * h + u[last:last + 1]
    h_ref[...] = h
    hs = jnp.concatenate(outs, axis=0)
    o_ref[...] = (hs * jax.nn.gelu(ya_ref[...])).astype(BF16)


def _lru(proj, layer, cw, cb, wa, ba, wx, bx, lam):
    nt = SEQ // LRU_TT
    vec = pl.BlockSpec((None, 1, LRU_WIDTH), lambda b, t: (layer, 0, 0))
    gw = pl.BlockSpec((None, LRU_WIDTH // V7X_MXU_DIM, V7X_MXU_DIM, V7X_MXU_DIM),
                      lambda b, t: (layer, 0, 0, 0))
    return pl.pallas_call(
        _lru_kernel,
        grid=(BATCH, nt),
        in_specs=[
            pl.BlockSpec((LRU_TT, LRU_WIDTH), lambda b, t: (b * nt + t, COL_XA)),
            pl.BlockSpec((LRU_TT, LRU_WIDTH), lambda b, t: (b * nt + t, COL_YA)),
            pl.BlockSpec((None, LRU_CONV, LRU_WIDTH), lambda b, t: (layer, 0, 0)),
            vec, gw, vec, gw, vec, vec,
        ],
        out_specs=pl.BlockSpec((LRU_TT, LRU_WIDTH), lambda b, t: (b * nt + t, 0)),
        out_shape=jax.ShapeDtypeStruct((TOKENS, LRU_WIDTH), BF16),
        scratch_shapes=[pltpu.VMEM((LRU_TT + V7X_SUBLANES, LRU_WIDTH), F32),
                        pltpu.VMEM((1, LRU_WIDTH), F32)],
        compiler_params=_params(("parallel", "arbitrary")),
        name="rg_lru",
    )(proj, proj, cw, cb, wa, ba, wx, bx, lam)


def _rope_kernel(pos_ref, inv_ref, cos_ref, sin_ref):
    ang = pos_ref[...].astype(F32) * inv_ref[...]
    cos_ref[...] = jnp.cos(ang)
    sin_ref[...] = jnp.sin(ang)


def _rope_tables(positions):
    half = RET_QK_DIM // 2
    inv_freq = jnp.power(ROPE_BASE, -jnp.arange(half, dtype=F32) / half).reshape(1, half)
    out = jax.ShapeDtypeStruct((SEQ, half), F32)
    return pl.pallas_call(_rope_kernel, out_shape=(out, out), name="rope_tables")(
        positions.reshape(SEQ, 1), inv_freq)


def _retention_decays():
    log_g = np.log1p(-np.power(2.0, -5.0 - np.arange(RET_HEADS, dtype=np.float64)))
    idx = np.arange(RET_L)
    dist = idx[:, None] - idx[None, :]
    same = (idx[:, None] // CHUNK) == (idx[None, :] // CHUNK)
    earlier = (idx[None, :] // CHUNK) < (idx[:, None] // CHUNK)
    expo = np.where(same, np.abs(dist), dist).astype(np.float64)
    intra = np.where(same | earlier, np.exp(log_g[:, None, None] * expo[None]), 0.0)
    q_decay = np.exp(log_g[:, None] * (idx[None, :] + 1.0))[..., None]
    k_decay = np.exp(log_g[:, None] * (RET_L - 1.0 - idx[None, :]))[..., None]
    block_decay = [float(v) for v in np.exp(log_g * RET_L)]
    return (jnp.asarray(intra, F32), jnp.asarray(q_decay, F32), jnp.asarray(k_decay, F32),
            block_decay)


def _ret_kernel(block_decay, q_ref, k_ref, v_ref, g_ref, cos_ref, sin_ref, dm_ref, qd_ref, kd_ref,
                o_ref, state_ref):
    @pl.when(pl.program_id(1) == 0)
    def _():
        state_ref[...] = jnp.zeros_like(state_ref)

    cos = cos_ref[...]
    sin = sin_ref[...]
    half = RET_QK_DIM // 2

    def rope(t):
        t1, t2 = t[:, :half], t[:, half:]
        return jnp.concatenate([t1 * cos - t2 * sin, t1 * sin + t2 * cos], axis=-1)

    for h in range(RET_HEADS):
        qs = slice(h * RET_QK_DIM, (h + 1) * RET_QK_DIM)
        vs = slice(h * RET_V_DIM, (h + 1) * RET_V_DIM)
        qr = rope(q_ref[:, qs])
        kr = rope(k_ref[:, qs]) * (RET_QK_DIM ** -0.5)
        qb = qr.astype(BF16)
        vb = v_ref[:, vs].astype(BF16)
        scores = lax.dot_general(qb, kr.astype(BF16), (((1,), (1,)), ((), ())),
                                 preferred_element_type=F32) * dm_ref[h]
        intra = jnp.dot(scores.astype(BF16), vb, preferred_element_type=F32)
        state = state_ref[h]
        cross = jnp.dot(qb, state.astype(BF16), preferred_element_type=F32) * qd_ref[h]
        kd = (kr * kd_ref[h]).astype(BF16)
        state_ref[h] = state * block_decay[h] + lax.dot_general(
            kd, vb, (((0,), (0,)), ((), ())), preferred_element_type=F32)
        out = intra + cross
        out = out * lax.rsqrt(jnp.mean(out * out, axis=-1, keepdims=True) + EPS)
        o_ref[:, vs] = (jax.nn.silu(g_ref[:, vs]) * out).astype(BF16)


def _retention(proj, cos, sin):
    nt = SEQ // RET_L
    intra, q_decay, k_decay, block_decay = _retention_decays()
    whole3 = lambda b, t: (0, 0, 0)
    return pl.pallas_call(
        functools.partial(_ret_kernel, block_decay),
        grid=(BATCH, nt),
        in_specs=[
            pl.BlockSpec((RET_L, RET_QK), lambda b, t: (b * nt + t, COL_Q)),
            pl.BlockSpec((RET_L, RET_QK), lambda b, t: (b * nt + t, COL_K)),
            pl.BlockSpec((RET_L, RET_V), lambda b, t: (b * nt + t, COL_V)),
            pl.BlockSpec((RET_L, RET_V), lambda b, t: (b * nt + t, COL_G)),
            pl.BlockSpec((RET_L, RET_QK_DIM // 2), lambda b, t: (t, 0)),
            pl.BlockSpec((RET_L, RET_QK_DIM // 2), lambda b, t: (t, 0)),
            pl.BlockSpec((RET_HEADS, RET_L, RET_L), whole3),
            pl.BlockSpec((RET_HEADS, RET_L, 1), whole3),
            pl.BlockSpec((RET_HEADS, RET_L, 1), whole3),
        ],
        out_specs=pl.BlockSpec((RET_L, RET_V), lambda b, t: (b * nt + t, 0)),
        out_shape=jax.ShapeDtypeStruct((TOKENS, RET_V), BF16),
        scratch_shapes=[pltpu.VMEM((RET_HEADS, RET_QK_DIM, RET_V_DIM), F32)],
        compiler_params=_params(("parallel", "arbitrary")),
        name="retention",
    )(proj, proj, proj, proj, cos, sin, intra, q_decay, k_decay)


def _out_kernel(x_ref, za_ref, zb_ref, sb_ref, sc_ref, sh_ref, ga_ref, gb_ref, gc_ref, cw_ref,
                wlo_ref, wro_ref, wso_ref, wout_ref, gpost_ref, o_ref, ext_ref):
    rows = x_ref.shape[0]

    @pl.when(pl.program_id(1) == 0)
    def _():
        ext_ref[0:V7X_SUBLANES, :] = jnp.zeros((V7X_SUBLANES, SC_WIDTH), F32)

    conv = _causal_taps(ext_ref, sc_ref[...] * sh_ref[...], cw_ref[...], rows)
    zc = (sb_ref[...] * conv).astype(BF16)
    oa = jnp.dot(za_ref[...], wlo_ref[...], preferred_element_type=F32)
    ob = jnp.dot(zb_ref[...], wro_ref[...], preferred_element_type=F32)
    oc = jnp.dot(zc, wso_ref[...], preferred_element_type=F32)
    mix = (jax.nn.sigmoid(ga_ref[...]) * oa + jax.nn.sigmoid(gb_ref[...]) * ob
           + jax.nn.sigmoid(gc_ref[...]) * oc)
    h = jnp.dot(mix.astype(BF16), wout_ref[...], preferred_element_type=F32)
    o_ref[...] = x_ref[...] + _rms(h, gpost_ref[...])


def _mix_out(x, za, zb, proj, layer, cw, wlo, wro, wso, wout, gpost):
    nt = SEQ // OUT_TT
    row = lambda b, t: b * nt + t

    def cols(c):
        return pl.BlockSpec((OUT_TT, D_MODEL), lambda b, t: (row(b, t), c))

    def weight(k):
        return pl.BlockSpec((None, k, D_MODEL), lambda b, t: (layer, 0, 0))

    return pl.pallas_call(
        _out_kernel,
        grid=(BATCH, nt),
        in_specs=[
            cols(0), cols(0),
            pl.BlockSpec((OUT_TT, RET_V), lambda b, t: (row(b, t), 0)),
            cols(COL_SB), cols(COL_SC), cols(COL_SH),
            cols(COL_GATES), cols(COL_GATES + 1), cols(COL_GATES + 2),
            weight(SC_CONV), weight(LRU_WIDTH), weight(RET_V), weight(SC_WIDTH), weight(D_MODEL),
            weight(1),
        ],
        out_specs=cols(0),
        out_shape=jax.ShapeDtypeStruct((TOKENS, D_MODEL), F32),
        scratch_shapes=[pltpu.VMEM((OUT_TT + V7X_SUBLANES, SC_WIDTH), F32)],
        compiler_params=_params(("parallel", "arbitrary")),
        name="mix_out",
    )(x, za, zb, proj, proj, proj, proj, proj, proj, cw, wlo, wro, wso, wout, gpost)


def _block_diag(w):
    per = V7X_MXU_DIM // LRU_BLOCK_DIM
    groups = LRU_BLOCKS // per
    w5 = w.reshape(DEPTH, groups, per, LRU_BLOCK_DIM, LRU_BLOCK_DIM)
    eye = jnp.eye(per, dtype=w.dtype)
    bd = jnp.einsum("lgaij,ab->lgaibj", w5, eye)
    return bd.reshape(DEPTH, groups, V7X_MXU_DIM, V7X_MXU_DIM).astype(BF16)


def kernel(x, positions, ffn1_pre_g, ffn1_w_in, ffn1_w_out, ffn1_post_g, mix_pre_g, w_mix_in, lru_conv_w, lru_conv_b, lru_w_a, lru_b_a, lru_w_x, lru_b_x, lru_lambda, w_lru_out, w_ret_out, sc_conv_w, w_sc_out, w_mix_out, mix_post_g, ffn2_pre_g, ffn2_w_in, ffn2_w_out, ffn2_post_g):
    vec = lambda v: v.reshape(DEPTH, 1, -1)
    bf = lambda w: w.astype(BF16)
    cos, sin = _rope_tables(positions)
    wa_bd, wx_bd = _block_diag(lru_w_a), _block_diag(lru_w_x)
    f1_in, f1_out, f2_in, f2_out = bf(ffn1_w_in), bf(ffn1_w_out), bf(ffn2_w_in), bf(ffn2_w_out)
    w_in, wlo, wro, wso, wout = bf(w_mix_in), bf(w_lru_out), bf(w_ret_out), bf(w_sc_out), bf(w_mix_out)

    h = x.reshape(TOKENS, D_MODEL)
    for l in range(DEPTH):
        h = _ffn(h, l, vec(ffn1_pre_g), f1_in, f1_out, vec(ffn1_post_g))
        proj = _proj(h, l, vec(mix_pre_g), w_in)
        za = _lru(proj, l, lru_conv_w, vec(lru_conv_b), wa_bd, vec(lru_b_a), wx_bd, vec(lru_b_x),
                  vec(lru_lambda))
        zb = _retention(proj, cos, sin)
        h = _mix_out(h, za, zb, proj, l, sc_conv_w, wlo, wro, wso, wout, vec(mix_post_g))
        h = _ffn(h, l, vec(ffn2_pre_g), f2_in, f2_out, vec(ffn2_post_g))
    return h.reshape(BATCH, SEQ, D_MODEL)
```

```python
import functools

import numpy as np
import jax
import jax.numpy as jnp
from jax import lax
from jax.experimental import pallas as pl
from jax.experimental.pallas import tpu as pltpu

D_MODEL = 1024
BATCH = 8
SEQ = 2048
DEPTH = 2
CHUNK = 64
EPS = 1e-6
LRU_WIDTH = D_MODEL
LRU_BLOCKS = 16
LRU_BLOCK_DIM = LRU_WIDTH // LRU_BLOCKS
LRU_CONV = 4
LRU_C = 8.0
RET_QK_DIM = 256
RET_V_DIM = 512
RET_HEADS = D_MODEL // RET_QK_DIM
RET_QK = RET_HEADS * RET_QK_DIM
RET_V = RET_HEADS * RET_V_DIM
ROPE_BASE = 10000.0
SC_WIDTH = D_MODEL
SC_CONV = 3
D_FF = 2816
MIX_IN = 2 * LRU_WIDTH + 2 * RET_QK + 2 * RET_V + 3 * SC_WIDTH + 3 * D_MODEL
TOKENS = BATCH * SEQ

V7X_MXU_DIM = 256
V7X_SUBLANES = 8
V7X_VMEM_LIMIT_BYTES = 56 * 1024 * 1024

COL_XA, COL_YA, COL_Q, COL_K = 0, 1, 2, 3
COL_V, COL_G = 2, 3
COL_SB, COL_SC, COL_SH, COL_GATES = 8, 9, 10, 11

FFN_TM = 512
FFN_TF = 512
PROJ_TM = 1024
PROJ_TN = 2048
LRU_TT = 256
RET_L = 256
OUT_TT = 512

BF16 = jnp.bfloat16
F32 = jnp.float32


def _params(semantics):
    return pltpu.CompilerParams(dimension_semantics=semantics,
                                vmem_limit_bytes=V7X_VMEM_LIMIT_BYTES)


def _rms(x, g):
    return x * lax.rsqrt(jnp.mean(x * x, axis=-1, keepdims=True) + EPS) * g


def _ffn_chunks():
    return [(c, min(FFN_TF, D_FF - c)) for c in range(0, D_FF, FFN_TF)]


def _ffn_kernel(x_ref, gpre_ref, win_ref, wout_ref, gpost_ref, o_ref, xn_ref, acc_ref):
    xn_ref[...] = _rms(x_ref[...], gpre_ref[...]).astype(BF16)
    xn = xn_ref[...]
    for n, (c, w) in enumerate(_ffn_chunks()):
        gate = jnp.dot(xn, win_ref[:, c:c + w], preferred_element_type=F32)
        up = jnp.dot(xn, win_ref[:, D_FF + c:D_FF + c + w], preferred_element_type=F32)
        act = (jax.nn.silu(gate) * up).astype(BF16)
        part = jnp.dot(act, wout_ref[c:c + w, :], preferred_element_type=F32)
        if n == 0:
            acc_ref[...] = part
        else:
            acc_ref[...] += part
    o_ref[...] = x_ref[...] + 0.5 * _rms(acc_ref[...], gpost_ref[...])


def _ffn(x, layer, gpre, w_in, w_out, gpost):
    resident = dict(pipeline_mode=pl.Buffered(1))
    return pl.pallas_call(
        _ffn_kernel,
        grid=(TOKENS // FFN_TM,),
        in_specs=[
            pl.BlockSpec((FFN_TM, D_MODEL), lambda i: (i, 0)),
            pl.BlockSpec((None, 1, D_MODEL), lambda i: (layer, 0, 0)),
            pl.BlockSpec((None, D_MODEL, 2 * D_FF), lambda i: (layer, 0, 0), **resident),
            pl.BlockSpec((None, D_FF, D_MODEL), lambda i: (layer, 0, 0), **resident),
            pl.BlockSpec((None, 1, D_MODEL), lambda i: (layer, 0, 0)),
        ],
        out_specs=pl.BlockSpec((FFN_TM, D_MODEL), lambda i: (i, 0)),
        out_shape=jax.ShapeDtypeStruct((TOKENS, D_MODEL), F32),
        scratch_shapes=[pltpu.VMEM((FFN_TM, D_MODEL), BF16), pltpu.VMEM((FFN_TM, D_MODEL), F32)],
        compiler_params=_params(("parallel",)),
        name="ffn",
    )(x, gpre, w_in, w_out, gpost)


def _proj_kernel(x_ref, g_ref, w_ref, o_ref, xn_ref):
    @pl.when(pl.program_id(1) == 0)
    def _():
        xn_ref[...] = _rms(x_ref[...], g_ref[...]).astype(BF16)

    o_ref[...] = jnp.dot(xn_ref[...], w_ref[...], preferred_element_type=F32)


def _proj(x, layer, g, w):
    return pl.pallas_call(
        _proj_kernel,
        grid=(TOKENS // PROJ_TM, MIX_IN // PROJ_TN),
        in_specs=[
            pl.BlockSpec((PROJ_TM, D_MODEL), lambda i, j: (i, 0)),
            pl.BlockSpec((None, 1, D_MODEL), lambda i, j: (layer, 0, 0)),
            pl.BlockSpec((None, D_MODEL, PROJ_TN), lambda i, j: (layer, 0, j)),
        ],
        out_specs=pl.BlockSpec((PROJ_TM, PROJ_TN), lambda i, j: (i, j)),
        out_shape=jax.ShapeDtypeStruct((TOKENS, MIX_IN), F32),
        scratch_shapes=[pltpu.VMEM((PROJ_TM, D_MODEL), BF16)],
        compiler_params=_params(("parallel", "arbitrary")),
        name="mix_in",
    )(x, g, w)


def _causal_taps(ext_ref, cur, w, rows):
    taps = w.shape[0]
    ext_ref[V7X_SUBLANES:V7X_SUBLANES + rows, :] = cur
    acc = cur * w[taps - 1:taps]
    for d in range(1, taps):
        acc = acc + ext_ref[V7X_SUBLANES - d:V7X_SUBLANES - d + rows, :] * w[taps - 1 - d:taps - d]
    ext_ref[0:V7X_SUBLANES, :] = cur[rows - V7X_SUBLANES:rows]
    return acc


def _lru_kernel(xa_ref, ya_ref, cw_ref, cb_ref, wa_ref, ba_ref, wx_ref, bx_ref, lam_ref,
                o_ref, ext_ref, h_ref):
    rows = xa_ref.shape[0]

    @pl.when(pl.program_id(1) == 0)
    def _():
        ext_ref[0:V7X_SUBLANES, :] = jnp.zeros((V7X_SUBLANES, LRU_WIDTH), F32)
        h_ref[...] = jnp.zeros_like(h_ref)

    xc = _causal_taps(ext_ref, xa_ref[...], cw_ref[...], rows) + cb_ref[...]
    xcb = xc.astype(BF16)
    groups = LRU_WIDTH // V7X_MXU_DIM

    def gate(w_ref, b_ref):
        parts = [jnp.dot(xcb[:, g * V7X_MXU_DIM:(g + 1) * V7X_MXU_DIM], w_ref[g],
                         preferred_element_type=F32) for g in range(groups)]
        return jax.nn.sigmoid(jnp.concatenate(parts, axis=-1) + b_ref[...])

    r = gate(wa_ref, ba_ref)
    i = gate(wx_ref, bx_ref)
    nlam = -lam_ref[...]
    softplus = jnp.maximum(nlam, 0.0) + jnp.log1p(jnp.exp(-jnp.abs(nlam)))
    log_a = r * (-LRU_C * softplus)
    a = jnp.exp(log_a)
    u = jnp.sqrt(-jnp.tanh(log_a) * (a * a + 1.0)) * (i * xc)

    rowmod = lax.broadcasted_iota(jnp.int32, a.shape, 0) & (V7X_SUBLANES - 1)
    d = 1
    while d < V7X_SUBLANES:
        keep = rowmod >= d
        a_prev = jnp.where(keep, pltpu.roll(a, d, axis=0), 1.0)
        u_prev = jnp.where(keep, pltpu.roll(u, d, axis=0), 0.0)
        u = a * u_prev + u
        a = a * a_prev
        d *= 2

    h = h_ref[...]
    outs = []
    for g in range(rows // V7X_SUBLANES):
        lo = g * V7X_SUBLANES
        outs.append(a[lo:lo + V7X_SUBLANES] * h + u[lo:lo + V7X_SUBLANES])
        last = lo + V7X_SUBLANES - 1
        h = a[last:last + 1] * h + u[last:last + 1]
    h_ref[...] = h
    hs = jnp.concatenate(outs, axis=0)
    o_ref[...] = (hs * jax.nn.gelu(ya_ref[...])).astype(BF16)


def _lru(proj, layer, cw, cb, wa, ba, wx, bx, lam):
    nt = SEQ // LRU_TT
    vec = pl.BlockSpec((None, 1, LRU_WIDTH), lambda b, t: (layer, 0, 0))
    gw = pl.BlockSpec((None, LRU_WIDTH // V7X_MXU_DIM, V7X_MXU_DIM, V7X_MXU_DIM),
                      lambda b, t: (layer, 0, 0, 0))
    return pl.pallas_call(
        _lru_kernel,
        grid=(BATCH, nt),
        in_specs=[
            pl.BlockSpec((LRU_TT, LRU_WIDTH), lambda b, t: (b * nt + t, COL_XA)),
            pl.BlockSpec((LRU_TT, LRU_WIDTH), lambda b, t: (b * nt + t, COL_YA)),
            pl.BlockSpec((None, LRU_CONV, LRU_WIDTH), lambda b, t: (layer, 0, 0)),
            vec, gw, vec, gw, vec, vec,
        ],
        out_specs=pl.BlockSpec((LRU_TT, LRU_WIDTH), lambda b, t: (b * nt + t, 0)),
        out_shape=jax.ShapeDtypeStruct((TOKENS, LRU_WIDTH), BF16),
        scratch_shapes=[pltpu.VMEM((LRU_TT + V7X_SUBLANES, LRU_WIDTH), F32),
                        pltpu.VMEM((1, LRU_WIDTH), F32)],
        compiler_params=_params(("parallel", "arbitrary")),
        name="rg_lru",
    )(proj, proj, cw, cb, wa, ba, wx, bx, lam)


def _rope_kernel(pos_ref, inv_ref, cos_ref, sin_ref):
    ang = pos_ref[...].astype(F32) * inv_ref[...]
    cos_ref[...] = jnp.cos(ang)
    sin_ref[...] = jnp.sin(ang)


def _rope_tables(positions):
    half = RET_QK_DIM // 2
    inv_freq = jnp.power(ROPE_BASE, -jnp.arange(half, dtype=F32) / half).reshape(1, half)
    out = jax.ShapeDtypeStruct((SEQ, half), F32)
    return pl.pallas_call(_rope_kernel, out_shape=(out, out), name="rope_tables")(
        positions.reshape(SEQ, 1), inv_freq)


def _retention_decays():
    log_g = np.log1p(-np.power(2.0, -5.0 - np.arange(RET_HEADS, dtype=np.float64)))
    idx = np.arange(RET_L)
    dist = idx[:, None] - idx[None, :]
    same = (idx[:, None] // CHUNK) == (idx[None, :] // CHUNK)
    earlier = (idx[None, :] // CHUNK) < (idx[:, None] // CHUNK)
    expo = np.where(same, np.abs(dist), dist).astype(np.float64)
    intra = np.where(same | earlier, np.exp(log_g[:, None, None] * expo[None]), 0.0)
    q_decay = np.exp(log_g[:, None] * (idx[None, :] + 1.0))[..., None]
    k_decay = np.exp(log_g[:, None] * (RET_L - 1.0 - idx[None, :]))[..., None]
    block_decay = [float(v) for v in np.exp(log_g * RET_L)]
    return (jnp.asarray(intra, F32), jnp.asarray(q_decay, F32), jnp.asarray(k_decay, F32),
            block_decay)


def _ret_kernel(block_decay, q_ref, k_ref, v_ref, g_ref, cos_ref, sin_ref, dm_ref, qd_ref, kd_ref,
                o_ref, state_ref):
    @pl.when(pl.program_id(1) == 0)
    def _():
        state_ref[...] = jnp.zeros_like(state_ref)

    cos = cos_ref[...]
    sin = sin_ref[...]
    half = RET_QK_DIM // 2

    def rope(t):
        t1, t2 = t[:, :half], t[:, half:]
        return jnp.concatenate([t1 * cos - t2 * sin, t1 * sin + t2 * cos], axis=-1)

    for h in range(RET_HEADS):
        qs = slice(h * RET_QK_DIM, (h + 1) * RET_QK_DIM)
        vs = slice(h * RET_V_DIM, (h + 1) * RET_V_DIM)
        qr = rope(q_ref[:, qs])
        kr = rope(k_ref[:, qs]) * (RET_QK_DIM ** -0.5)
        qb = qr.astype(BF16)
        vb = v_ref[:, vs].astype(BF16)
        scores = lax.dot_general(qb, kr.astype(BF16), (((1,), (1,)), ((), ())),
                                 preferred_element_type=F32) * dm_ref[h]
        intra = jnp.dot(scores.astype(BF16), vb, preferred_element_type=F32)
        state = state_ref[h]
        cross = jnp.dot(qb, state.astype(BF16), preferred_element_type=F32) * qd_ref[h]
        kd = (kr * kd_ref[h]).astype(BF16)
        state_ref[h] = state * block_decay[h] + lax.dot_general(
            kd, vb, (((0,), (0,)), ((), ())), preferred_element_type=F32)
        out = intra + cross
        out = out * lax.rsqrt(jnp.mean(out * out, axis=-1, keepdims=True) + EPS)
        o_ref[:, vs] = (jax.nn.silu(g_ref[:, vs]) * out).astype(BF16)


def _retention(proj, cos, sin):
    nt = SEQ // RET_L
    intra, q_decay, k_decay, block_decay = _retention_decays()
    whole3 = lambda b, t: (0, 0, 0)
    return pl.pallas_call(
        functools.partial(_ret_kernel, block_decay),
        grid=(BATCH, nt),
        in_specs=[
            pl.BlockSpec((RET_L, RET_QK), lambda b, t: (b * nt + t, COL_Q)),
            pl.BlockSpec((RET_L, RET_QK), lambda b, t: (b * nt + t, COL_K)),
            pl.BlockSpec((RET_L, RET_V), lambda b, t: (b * nt + t, COL_V)),
            pl.BlockSpec((RET_L, RET_V), lambda b, t: (b * nt + t, COL_G)),
            pl.BlockSpec((RET_L, RET_QK_DIM // 2), lambda b, t: (t, 0)),
            pl.BlockSpec((RET_L, RET_QK_DIM // 2), lambda b, t: (t, 0)),
            pl.BlockSpec((RET_HEADS, RET_L, RET_L), whole3),
            pl.BlockSpec((RET_HEADS, RET_L, 1), whole3),
            pl.BlockSpec((RET_HEADS, RET_L, 1), whole3),
        ],
        out_specs=pl.BlockSpec((RET_L, RET_V), lambda b, t: (b * nt + t, 0)),
        out_shape=jax.ShapeDtypeStruct((TOKENS, RET_V), BF16),
        scratch_shapes=[pltpu.VMEM((RET_HEADS, RET_QK_DIM, RET_V_DIM), F32)],
        compiler_params=_params(("parallel", "arbitrary")),
        name="retention",
    )(proj, proj, proj, proj, cos, sin, intra, q_decay, k_decay)


def _out_kernel(x_ref, za_ref, zb_ref, sb_ref, sc_ref, sh_ref, ga_ref, gb_ref, gc_ref, cw_ref,
                wlo_ref, wro_ref, wso_ref, wout_ref, gpost_ref, o_ref, ext_ref):
    rows = x_ref.shape[0]

    @pl.when(pl.program_id(1) == 0)
    def _():
        ext_ref[0:V7X_SUBLANES, :] = jnp.zeros((V7X_SUBLANES, SC_WIDTH), F32)

    conv = _causal_taps(ext_ref, sc_ref[...] * sh_ref[...], cw_ref[...], rows)
    zc = (sb_ref[...] * conv).astype(BF16)
    oa = jnp.dot(za_ref[...], wlo_ref[...], preferred_element_type=F32)
    ob = jnp.dot(zb_ref[...], wro_ref[...], preferred_element_type=F32)
    oc = jnp.dot(zc, wso_ref[...], preferred_element_type=F32)
    mix = (jax.nn.sigmoid(ga_ref[...]) * oa + jax.nn.sigmoid(gb_ref[...]) * ob
           + jax.nn.sigmoid(gc_ref[...]) * oc)
    h = jnp.dot(mix.astype(BF16), wout_ref[...], preferred_element_type=F32)
    o_ref[...] = x_ref[...] + _rms(h, gpost_ref[...])


def _mix_out(x, za, zb, proj, layer, cw, wlo, wro, wso, wout, gpost):
    nt = SEQ // OUT_TT
    row = lambda b, t: b * nt + t

    def cols(c):
        return pl.BlockSpec((OUT_TT, D_MODEL), lambda b, t: (row(b, t), c))

    def weight(k):
        return pl.BlockSpec((None, k, D_MODEL), lambda b, t: (layer, 0, 0))

    return pl.pallas_call(
        _out_kernel,
        grid=(BATCH, nt),
        in_specs=[
            cols(0), cols(0),
            pl.BlockSpec((OUT_TT, RET_V), lambda b, t: (row(b, t), 0)),
            cols(COL_SB), cols(COL_SC), cols(COL_SH),
            cols(COL_GATES), cols(COL_GATES + 1), cols(COL_GATES + 2),
            weight(SC_CONV), weight(LRU_WIDTH), weight(RET_V), weight(SC_WIDTH), weight(D_MODEL),
            weight(1),
        ],
        out_specs=cols(0),
        out_shape=jax.ShapeDtypeStruct((TOKENS, D_MODEL), F32),
        scratch_shapes=[pltpu.VMEM((OUT_TT + V7X_SUBLANES, SC_WIDTH), F32)],
        compiler_params=_params(("parallel", "arbitrary")),
        name="mix_out",
    )(x, za, zb, proj, proj, proj, proj, proj, proj, cw, wlo, wro, wso, wout, gpost)


def _block_diag(w):
    per = V7X_MXU_DIM // LRU_BLOCK_DIM
    groups = LRU_BLOCKS // per
    w5 = w.reshape(DEPTH, groups, per, LRU_BLOCK_DIM, LRU_BLOCK_DIM)
    eye = jnp.eye(per, dtype=w.dtype)
    bd = jnp.einsum("lgaij,ab->lgaibj", w5, eye)
    return bd.reshape(DEPTH, groups, V7X_MXU_DIM, V7X_MXU_DIM).astype(BF16)


def kernel(x, positions, ffn1_pre_g, ffn1_w_in, ffn1_w_out, ffn1_post_g, mix_pre_g, w_mix_in, lru_conv_w, lru_conv_b, lru_w_a, lru_b_a, lru_w_x, lru_b_x, lru_lambda, w_lru_out, w_ret_out, sc_conv_w, w_sc_out, w_mix_out, mix_post_g, ffn2_pre_g, ffn2_w_in, ffn2_w_out, ffn2_post_g):
    vec = lambda v: v.reshape(DEPTH, 1, -1)
    bf = lambda w: w.astype(BF16)
    cos, sin = _rope_tables(positions)
    wa_bd, wx_bd = _block_diag(lru_w_a), _block_diag(lru_w_x)
    f1_in, f1_out, f2_in, f2_out = bf(ffn1_w_in), bf(ffn1_w_out), bf(ffn2_w_in), bf(ffn2_w_out)
    w_in, wlo, wro, wso, wout = bf(w_mix_in), bf(w_lru_out), bf(w_ret_out), bf(w_sc_out), bf(w_mix_out)

    h = x.reshape(TOKENS, D_MODEL)
    for l in range(DEPTH):
        h = _ffn(h, l, vec(ffn1_pre_g), f1_in, f1_out, vec(ffn1_post_g))
        proj = _proj(h, l, vec(mix_pre_g), w_in)
        za = _lru(proj, l, lru_conv_w, vec(lru_conv_b), wa_bd, vec(lru_b_a), wx_bd, vec(lru_b_x),
                  vec(lru_lambda))
        zb = _retention(proj, cos, sin)
        h = _mix_out(h, za, zb, proj, l, sc_conv_w, wlo, wro, wso, wout, vec(mix_post_g))
        h = _ffn(h, l, vec(ffn2_pre_g), f2_in, f2_out, vec(ffn2_post_g))
    return h.reshape(BATCH, SEQ, D_MODEL)
```

```python
import functools

import numpy as np
import jax
import jax.numpy as jnp
from jax import lax
from jax.experimental import pallas as pl
from jax.experimental.pallas import tpu as pltpu

D_MODEL = 1024
BATCH = 8
SEQ = 2048
DEPTH = 2
CHUNK = 64
EPS = 1e-6
LRU_WIDTH = D_MODEL
LRU_BLOCKS = 16
LRU_BLOCK_DIM = LRU_WIDTH // LRU_BLOCKS
LRU_CONV = 4
LRU_C = 8.0
RET_QK_DIM = 256
RET_V_DIM = 512
RET_HEADS = D_MODEL // RET_QK_DIM
RET_QK = RET_HEADS * RET_QK_DIM
RET_V = RET_HEADS * RET_V_DIM
ROPE_BASE = 10000.0
SC_WIDTH = D_MODEL
SC_CONV = 3
D_FF = 2816
TOKENS = BATCH * SEQ

LRU_IN = 2 * LRU_WIDTH
RET_IN = 2 * RET_QK + 2 * RET_V
OUT_IN = 3 * SC_WIDTH + 3 * D_MODEL
MIX_IN = LRU_IN + RET_IN + OUT_IN

V7X_MXU_DIM = 256
V7X_SUBLANES = 8
V7X_VMEM_LIMIT_BYTES = 56 * 1024 * 1024

FFN_TM = 512
FFN_TF = 512
MIX_TT = 512
RET_L = 256

BF16 = jnp.bfloat16
F32 = jnp.float32
RESIDENT = dict(pipeline_mode=pl.Buffered(1))


def _params(semantics):
    return pltpu.CompilerParams(dimension_semantics=semantics,
                                vmem_limit_bytes=V7X_VMEM_LIMIT_BYTES)


def _rms(x, g):
    return x * lax.rsqrt(jnp.mean(x * x, axis=-1, keepdims=True) + EPS) * g


def _dot(a, b):
    return jnp.dot(a, b, preferred_element_type=F32)


def _layer_block(shape, layer, **kw):
    return pl.BlockSpec((None,) + shape, lambda *_: (layer,) + (0,) * len(shape), **kw)


def _ffn_chunks():
    return [(c, min(FFN_TF, D_FF - c)) for c in range(0, D_FF, FFN_TF)]


def _ffn_kernel(emit_norm, x_ref, gpre_ref, win_ref, wout_ref, gpost_ref, *rest):
    if emit_norm:
        gnext_ref, o_ref, on_ref, xn_ref, acc_ref = rest
    else:
        o_ref, xn_ref, acc_ref = rest
    xn_ref[...] = _rms(x_ref[...], gpre_ref[...]).astype(BF16)
    xn = xn_ref[...]
    for n, (c, w) in enumerate(_ffn_chunks()):
        gate = _dot(xn, win_ref[:, c:c + w])
        up = _dot(xn, win_ref[:, D_FF + c:D_FF + c + w])
        act = (jax.nn.silu(gate) * up).astype(BF16)
        part = _dot(act, wout_ref[c:c + w, :])
        if n == 0:
            acc_ref[...] = part
        else:
            acc_ref[...] += part
    y = x_ref[...] + 0.5 * _rms(acc_ref[...], gpost_ref[...])
    o_ref[...] = y
    if emit_norm:
        on_ref[...] = _rms(y, gnext_ref[...]).astype(BF16)


def _ffn(x, layer, gpre, w_in, w_out, gpost, gnext=None):
    emit_norm = gnext is not None
    rows = pl.BlockSpec((FFN_TM, D_MODEL), lambda i: (i, 0))
    vec = _layer_block((1, D_MODEL), layer)
    in_specs = [rows, vec, _layer_block((D_MODEL, 2 * D_FF), layer, **RESIDENT),
                _layer_block((D_FF, D_MODEL), layer, **RESIDENT), vec]
    args = [x, gpre, w_in, w_out, gpost]
    out_shape = jax.ShapeDtypeStruct((TOKENS, D_MODEL), F32)
    out_specs = rows
    if emit_norm:
        in_specs.append(vec)
        args.append(gnext)
        out_shape = (out_shape, jax.ShapeDtypeStruct((TOKENS, D_MODEL), BF16))
        out_specs = (rows, rows)
    return pl.pallas_call(
        functools.partial(_ffn_kernel, emit_norm),
        grid=(TOKENS // FFN_TM,),
        in_specs=in_specs,
        out_specs=out_specs,
        out_shape=out_shape,
        scratch_shapes=[pltpu.VMEM((FFN_TM, D_MODEL), BF16), pltpu.VMEM((FFN_TM, D_MODEL), F32)],
        compiler_params=_params(("parallel",)),
        name="ffn",
    )(*args)


_NT = SEQ // MIX_TT


def _tile(cols):
    return pl.BlockSpec((MIX_TT, cols), lambda b, t: (b * _NT + t, 0))


def _causal_taps(ext_ref, cur, w, rows):
    taps = w.shape[0]
    ext_ref[V7X_SUBLANES:V7X_SUBLANES + rows, :] = cur
    acc = cur * w[taps - 1:taps]
    for d in range(1, taps):
        acc = acc + ext_ref[V7X_SUBLANES - d:V7X_SUBLANES - d + rows, :] * w[taps - 1 - d:taps - d]
    ext_ref[0:V7X_SUBLANES, :] = cur[rows - V7X_SUBLANES:rows]
    return acc


def _lru_kernel(xn_ref, win_ref, cw_ref, cb_ref, wa_ref, ba_ref, wx_ref, bx_ref, lam_ref,
                o_ref, ext_ref, h_ref):
    rows = xn_ref.shape[0]

    @pl.when(pl.program_id(1) == 0)
    def _():
        ext_ref[0:V7X_SUBLANES, :] = jnp.zeros((V7X_SUBLANES, LRU_WIDTH), F32)
        h_ref[...] = jnp.zeros_like(h_ref)

    xn = xn_ref[...]
    xa = _dot(xn, win_ref[:, :LRU_WIDTH])
    xc = _causal_taps(ext_ref, xa, cw_ref[...], rows) + cb_ref[...]
    xcb = xc.astype(BF16)
    groups = LRU_WIDTH // V7X_MXU_DIM

    def gate(w_ref, b_ref):
        parts = [_dot(xcb[:, g * V7X_MXU_DIM:(g + 1) * V7X_MXU_DIM], w_ref[g])
                 for g in range(groups)]
        return jax.nn.sigmoid(jnp.concatenate(parts, axis=-1) + b_ref[...])

    r = gate(wa_ref, ba_ref)
    i = gate(wx_ref, bx_ref)
    nlam = -lam_ref[...]
    softplus = jnp.maximum(nlam, 0.0) + jnp.log1p(jnp.exp(-jnp.abs(nlam)))
    log_a = r * (-LRU_C * softplus)
    a = jnp.exp(log_a)
    u = jnp.sqrt(-jnp.tanh(log_a) * (a * a + 1.0)) * (i * xc)

    rowmod = lax.broadcasted_iota(jnp.int32, a.shape, 0) & (V7X_SUBLANES - 1)
    d = 1
    while d < V7X_SUBLANES:
        keep = rowmod >= d
        a_prev = jnp.where(keep, pltpu.roll(a, d, axis=0), 1.0)
        u_prev = jnp.where(keep, pltpu.roll(u, d, axis=0), 0.0)
        u = a * u_prev + u
        a = a * a_prev
        d *= 2

    h = h_ref[...]
    outs = []
    for g in range(rows // V7X_SUBLANES):
        lo = g * V7X_SUBLANES
        outs.append(a[lo:lo + V7X_SUBLANES] * h + u[lo:lo + V7X_SUBLANES])
        last = lo + V7X_SUBLANES - 1
        h = a[last:last + 1] * h + u[last:last + 1]
    h_ref[...] = h
    hs = jnp.concatenate(outs, axis=0)
    ya = _dot(xn, win_ref[:, LRU_WIDTH:])
    o_ref[...] = (hs * jax.nn.gelu(ya)).astype(BF16)


def _lru(xn, layer, w_in, cw, cb, wa, ba, wx, bx, lam):
    vec = _layer_block((1, LRU_WIDTH), layer)
    gw = _layer_block((LRU_WIDTH // V7X_MXU_DIM, V7X_MXU_DIM, V7X_MXU_DIM), layer)
    return pl.pallas_call(
        _lru_kernel,
        grid=(BATCH, _NT),
        in_specs=[_tile(D_MODEL), _layer_block((D_MODEL, LRU_IN), layer, **RESIDENT),
                  _layer_block((LRU_CONV, LRU_WIDTH), layer), vec, gw, vec, gw, vec, vec],
        out_specs=_tile(LRU_WIDTH),
        out_shape=jax.ShapeDtypeStruct((TOKENS, LRU_WIDTH), BF16),
        scratch_shapes=[pltpu.VMEM((MIX_TT + V7X_SUBLANES, LRU_WIDTH), F32),
                        pltpu.VMEM((1, LRU_WIDTH), F32)],
        compiler_params=_params(("parallel", "arbitrary")),
        name="rg_lru",
    )(xn, w_in, cw, cb, wa, ba, wx, bx, lam)


def _rope_kernel(pos_ref, inv_ref, cos_ref, sin_ref):
    ang = pos_ref[...].astype(F32) * inv_ref[...]
    cos_ref[...] = jnp.cos(ang)
    sin_ref[...] = jnp.sin(ang)


def _rope_tables(positions):
    half = RET_QK_DIM // 2
    inv_freq = jnp.power(ROPE_BASE, -jnp.arange(half, dtype=F32) / half).reshape(1, half)
    out = jax.ShapeDtypeStruct((SEQ, half), F32)
    return pl.pallas_call(_rope_kernel, out_shape=(out, out), name="rope_tables")(
        positions.reshape(SEQ, 1), inv_freq)


def _retention_decays():
    log_g = np.log1p(-np.power(2.0, -5.0 - np.arange(RET_HEADS, dtype=np.float64)))
    idx = np.arange(RET_L)
    dist = idx[:, None] - idx[None, :]
    same = (idx[:, None] // CHUNK) == (idx[None, :] // CHUNK)
    earlier = (idx[None, :] // CHUNK) < (idx[:, None] // CHUNK)
    expo = np.where(same, np.abs(dist), dist).astype(np.float64)
    intra = np.where(same | earlier, np.exp(log_g[:, None, None] * expo[None]), 0.0)
    q_decay = np.exp(log_g[:, None] * (idx[None, :] + 1.0))[..., None]
    k_decay = np.exp(log_g[:, None] * (RET_L - 1.0 - idx[None, :]))[..., None]
    block_decay = [float(v) for v in np.exp(log_g * RET_L)]
    return (jnp.asarray(intra, F32), jnp.asarray(q_decay, F32), jnp.asarray(k_decay, F32),
            block_decay)


def _ret_kernel(block_decay, xn_ref, win_ref, cos_ref, sin_ref, dm_ref, qd_ref, kd_ref,
                o_ref, state_ref):
    @pl.when(pl.program_id(1) == 0)
    def _():
        state_ref[...] = jnp.zeros_like(state_ref)

    xn = xn_ref[...]
    cos = cos_ref[...]
    sin = sin_ref[...]
    half = RET_QK_DIM // 2

    def rope(t):
        t1, t2 = t[:, :half], t[:, half:]
        return jnp.concatenate([t1 * cos - t2 * sin, t1 * sin + t2 * cos], axis=-1)

    for h in range(RET_HEADS):
        qc = h * RET_QK_DIM
        vc = 2 * RET_QK + h * RET_V_DIM
        vs = slice(h * RET_V_DIM, (h + 1) * RET_V_DIM)
        kr = rope(_dot(xn, win_ref[:, RET_QK + qc:RET_QK + qc + RET_QK_DIM])) * (RET_QK_DIM ** -0.5)
        qb = rope(_dot(xn, win_ref[:, qc:qc + RET_QK_DIM])).astype(BF16)
        kb = kr.astype(BF16)
        vb = _dot(xn, win_ref[:, vc:vc + RET_V_DIM]).astype(BF16)
        g = _dot(xn, win_ref[:, RET_V + vc:RET_V + vc + RET_V_DIM])
        for s in range(MIX_TT // RET_L):
            rs = slice(s * RET_L, (s + 1) * RET_L)
            scores = lax.dot_general(qb[rs], kb[rs], (((1,), (1,)), ((), ())),
                                     preferred_element_type=F32) * dm_ref[h]
            intra = _dot(scores.astype(BF16), vb[rs])
            state = state_ref[h]
            cross = _dot(qb[rs], state.astype(BF16)) * qd_ref[h]
            kd = (kr[rs] * kd_ref[h]).astype(BF16)
            state_ref[h] = state * block_decay[h] + lax.dot_general(
                kd, vb[rs], (((0,), (0,)), ((), ())), preferred_element_type=F32)
            out = intra + cross
            out = out * lax.rsqrt(jnp.mean(out * out, axis=-1, keepdims=True) + EPS)
            o_ref[rs, vs] = (jax.nn.silu(g[rs]) * out).astype(BF16)


def _retention(xn, layer, w_in, cos, sin):
    intra, q_decay, k_decay, block_decay = _retention_decays()
    whole3 = lambda b, t: (0, 0, 0)
    rot = pl.BlockSpec((MIX_TT, RET_QK_DIM // 2), lambda b, t: (t, 0))
    return pl.pallas_call(
        functools.partial(_ret_kernel, block_decay),
        grid=(BATCH, _NT),
        in_specs=[
            _tile(D_MODEL), _layer_block((D_MODEL, RET_IN), layer, **RESIDENT), rot, rot,
            pl.BlockSpec((RET_HEADS, RET_L, RET_L), whole3, **RESIDENT),
            pl.BlockSpec((RET_HEADS, RET_L, 1), whole3, **RESIDENT),
            pl.BlockSpec((RET_HEADS, RET_L, 1), whole3, **RESIDENT),
        ],
        out_specs=_tile(RET_V),
        out_shape=jax.ShapeDtypeStruct((TOKENS, RET_V), BF16),
        scratch_shapes=[pltpu.VMEM((RET_HEADS, RET_QK_DIM, RET_V_DIM), F32)],
        compiler_params=_params(("parallel", "arbitrary")),
        name="retention",
    )(xn, w_in, cos, sin, intra, q_decay, k_decay)


def _out_kernel(x_ref, xn_ref, za_ref, zb_ref, win_ref, cw_ref, wlo_ref, wro_ref, wso_ref,
                wout_ref, gpost_ref, o_ref, ext_ref):
    rows = x_ref.shape[0]

    @pl.when(pl.program_id(1) == 0)
    def _():
        ext_ref[0:V7X_SUBLANES, :] = jnp.zeros((V7X_SUBLANES, SC_WIDTH), F32)

    xn = xn_ref[...]

    def proj(c):
        return _dot(xn, win_ref[:, c * D_MODEL:(c + 1) * D_MODEL])

    conv = _causal_taps(ext_ref, proj(1) * proj(2), cw_ref[...], rows)
    zc = (proj(0) * conv).astype(BF16)
    mix = jax.nn.sigmoid(proj(3)) * _dot(za_ref[...], wlo_ref[...])
    mix = mix + jax.nn.sigmoid(proj(4)) * _dot(zb_ref[...], wro_ref[...])
    mix = mix + jax.nn.sigmoid(proj(5)) * _dot(zc, wso_ref[...])
    h = _dot(mix.astype(BF16), wout_ref[...])
    o_ref[...] = x_ref[...] + _rms(h, gpost_ref[...])


def _mix_out(x, xn, za, zb, layer, w_in, cw, wlo, wro, wso, wout, gpost):
    def weight(k):
        return _layer_block((k, D_MODEL), layer, **RESIDENT)

    return pl.pallas_call(
        _out_kernel,
        grid=(BATCH, _NT),
        in_specs=[
            _tile(D_MODEL), _tile(D_MODEL), _tile(LRU_WIDTH), _tile(RET_V),
            _layer_block((D_MODEL, OUT_IN), layer, **RESIDENT),
            _layer_block((SC_CONV, SC_WIDTH), layer),
            weight(LRU_WIDTH), weight(RET_V), weight(SC_WIDTH), weight(D_MODEL),
            _layer_block((1, D_MODEL), layer),
        ],
        out_specs=_tile(D_MODEL),
        out_shape=jax.ShapeDtypeStruct((TOKENS, D_MODEL), F32),
        scratch_shapes=[pltpu.VMEM((MIX_TT + V7X_SUBLANES, SC_WIDTH), F32)],
        compiler_params=_params(("parallel", "arbitrary")),
        name="mix_out",
    )(x, xn, za, zb, w_in, cw, wlo, wro, wso, wout, gpost)


def _block_diag(w):
    per = V7X_MXU_DIM // LRU_BLOCK_DIM
    groups = LRU_BLOCKS // per
    w5 = w.reshape(DEPTH, groups, per, LRU_BLOCK_DIM, LRU_BLOCK_DIM)
    eye = jnp.eye(per, dtype=w.dtype)
    bd = jnp.einsum("lgaij,ab->lgaibj", w5, eye)
    return bd.reshape(DEPTH, groups, V7X_MXU_DIM, V7X_MXU_DIM).astype(BF16)


def kernel(x, positions, ffn1_pre_g, ffn1_w_in, ffn1_w_out, ffn1_post_g, mix_pre_g, w_mix_in, lru_conv_w, lru_conv_b, lru_w_a, lru_b_a, lru_w_x, lru_b_x, lru_lambda, w_lru_out, w_ret_out, sc_conv_w, w_sc_out, w_mix_out, mix_post_g, ffn2_pre_g, ffn2_w_in, ffn2_w_out, ffn2_post_g):
    vec = lambda v: v.reshape(DEPTH, 1, -1)
    bf = lambda w: w.astype(BF16)
    cos, sin = _rope_tables(positions)
    wa_bd, wx_bd = _block_diag(lru_w_a), _block_diag(lru_w_x)
    f1_in, f1_out, f2_in, f2_out = bf(ffn1_w_in), bf(ffn1_w_out), bf(ffn2_w_in), bf(ffn2_w_out)
    w_lru_in = bf(w_mix_in[:, :, :LRU_IN])
    w_ret_in = bf(w_mix_in[:, :, LRU_IN:LRU_IN + RET_IN])
    w_out_in = bf(w_mix_in[:, :, LRU_IN + RET_IN:])
    wlo, wro, wso, wout = bf(w_lru_out), bf(w_ret_out), bf(w_sc_out), bf(w_mix_out)

    h = x.reshape(TOKENS, D_MODEL)
    for l in range(DEPTH):
        h, hn = _ffn(h, l, vec(ffn1_pre_g), f1_in, f1_out, vec(ffn1_post_g), vec(mix_pre_g))
        za = _lru(hn, l, w_lru_in, lru_conv_w, vec(lru_conv_b), wa_bd, vec(lru_b_a), wx_bd,
                  vec(lru_b_x), vec(lru_lambda))
        zb = _retention(hn, l, w_ret_in, cos, sin)
        h = _mix_out(h, hn, za, zb, l, w_out_in, sc_conv_w, wlo, wro, wso, wout, vec(mix_post_g))
        h = _ffn(h, l, vec(ffn2_pre_g), f2_in, f2_out, vec(ffn2_post_g))
    return h.reshape(BATCH, SEQ, D_MODEL)
```

```python
import functools

import numpy as np
import jax
import jax.numpy as jnp
from jax import lax
from jax.experimental import pallas as pl
from jax.experimental.pallas import tpu as pltpu

D_MODEL = 1024
BATCH = 8
SEQ = 2048
DEPTH = 2
CHUNK = 64
EPS = 1e-6
LRU_WIDTH = D_MODEL
LRU_BLOCKS = 16
LRU_BLOCK_DIM = LRU_WIDTH // LRU_BLOCKS
LRU_CONV = 4
LRU_C = 8.0
RET_QK_DIM = 256
RET_V_DIM = 512
RET_HEADS = D_MODEL // RET_QK_DIM
RET_QK = RET_HEADS * RET_QK_DIM
RET_V = RET_HEADS * RET_V_DIM
ROPE_BASE = 10000.0
SC_WIDTH = D_MODEL
SC_CONV = 3
D_FF = 2816
TOKENS = BATCH * SEQ

LRU_IN = 2 * LRU_WIDTH
RET_IN = 2 * RET_QK + 2 * RET_V
OUT_IN = 3 * SC_WIDTH + 3 * D_MODEL
MIX_IN = LRU_IN + RET_IN + OUT_IN

V7X_MXU_DIM = 256
V7X_SUBLANES = 8
V7X_VMEM_LIMIT_BYTES = 56 * 1024 * 1024

FFN_TM = 1024
FFN_TF = 512
MIX_TT = 512
RET_L = 256
LRU_RB = 64

BF16 = jnp.bfloat16
F32 = jnp.float32
RESIDENT = dict(pipeline_mode=pl.Buffered(1))


def _params(semantics):
    return pltpu.CompilerParams(dimension_semantics=semantics,
                                vmem_limit_bytes=V7X_VMEM_LIMIT_BYTES)


def _rms(x, g):
    return x * lax.rsqrt(jnp.mean(x * x, axis=-1, keepdims=True) + EPS) * g


def _dot(a, b):
    return jnp.dot(a, b, preferred_element_type=F32)


def _layer_block(shape, layer, **kw):
    return pl.BlockSpec((None,) + shape, lambda *_: (layer,) + (0,) * len(shape), **kw)


def _ffn_chunks():
    return [(c, min(FFN_TF, D_FF - c)) for c in range(0, D_FF, FFN_TF)]


def _ffn_kernel(emit_norm, x_ref, gpre_ref, win_ref, wout_ref, gpost_ref, *rest):
    if emit_norm:
        gnext_ref, o_ref, on_ref, xn_ref, acc_ref = rest
    else:
        o_ref, xn_ref, acc_ref = rest
    xn_ref[...] = _rms(x_ref[...], gpre_ref[...]).astype(BF16)
    xn = xn_ref[...]
    for n, (c, w) in enumerate(_ffn_chunks()):
        gate = _dot(xn, win_ref[:, c:c + w])
        up = _dot(xn, win_ref[:, D_FF + c:D_FF + c + w])
        act = (jax.nn.silu(gate) * up).astype(BF16)
        part = _dot(act, wout_ref[c:c + w, :])
        if n == 0:
            acc_ref[...] = part
        else:
            acc_ref[...] += part
    y = x_ref[...] + 0.5 * _rms(acc_ref[...], gpost_ref[...])
    o_ref[...] = y
    if emit_norm:
        on_ref[...] = _rms(y, gnext_ref[...]).astype(BF16)


def _ffn(x, layer, gpre, w_in, w_out, gpost, gnext=None):
    emit_norm = gnext is not None
    rows = pl.BlockSpec((FFN_TM, D_MODEL), lambda i: (i, 0))
    vec = _layer_block((1, D_MODEL), layer)
    in_specs = [rows, vec, _layer_block((D_MODEL, 2 * D_FF), layer, **RESIDENT),
                _layer_block((D_FF, D_MODEL), layer, **RESIDENT), vec]
    args = [x, gpre, w_in, w_out, gpost]
    out_shape = jax.ShapeDtypeStruct((TOKENS, D_MODEL), F32)
    out_specs = rows
    if emit_norm:
        in_specs.append(vec)
        args.append(gnext)
        out_shape = (out_shape, jax.ShapeDtypeStruct((TOKENS, D_MODEL), BF16))
        out_specs = (rows, rows)
    return pl.pallas_call(
        functools.partial(_ffn_kernel, emit_norm),
        grid=(TOKENS // FFN_TM,),
        in_specs=in_specs,
        out_specs=out_specs,
        out_shape=out_shape,
        scratch_shapes=[pltpu.VMEM((FFN_TM, D_MODEL), BF16), pltpu.VMEM((FFN_TM, D_MODEL), F32)],
        compiler_params=_params(("parallel",)),
        name="ffn",
    )(*args)


_NT = SEQ // MIX_TT


def _tile(cols):
    return pl.BlockSpec((MIX_TT, cols), lambda b, t: (b * _NT + t, 0))


def _causal_taps(ext_ref, cs, cur, w):
    taps, rows = w.shape[0], cur.shape[0]
    ext_ref[V7X_SUBLANES:V7X_SUBLANES + rows, cs] = cur
    acc = cur * w[taps - 1:taps]
    for d in range(1, taps):
        acc = acc + ext_ref[V7X_SUBLANES - d:V7X_SUBLANES - d + rows, cs] * w[taps - 1 - d:taps - d]
    ext_ref[0:V7X_SUBLANES, cs] = cur[rows - V7X_SUBLANES:rows]
    return acc


def _lru_pieces(g, xn, win_ref, cw_ref, cb_ref, wa_ref, ba_ref, wx_ref, bx_ref, lam_ref,
                o_ref, ext_ref, xc_ref, h_ref):
    rows = xn.shape[0]
    sub = V7X_SUBLANES
    cs = slice(g * V7X_MXU_DIM, (g + 1) * V7X_MXU_DIM)
    ys = slice(LRU_WIDTH + g * V7X_MXU_DIM, LRU_WIDTH + (g + 1) * V7X_MXU_DIM)
    st = {}

    def project():
        ext_ref[sub:sub + rows, cs] = _dot(xn, win_ref[:, cs])

    def conv(b):
        lo = sub + b * LRU_RB
        w = cw_ref[:, cs]
        acc = ext_ref[lo:lo + LRU_RB, cs] * w[LRU_CONV - 1:LRU_CONV]
        for d in range(1, LRU_CONV):
            acc = acc + ext_ref[lo - d:lo - d + LRU_RB, cs] * w[LRU_CONV - 1 - d:LRU_CONV - d]
        xc_ref[b * LRU_RB:(b + 1) * LRU_RB, cs] = acc + cb_ref[:, cs]

    def gates():
        ext_ref[0:sub, cs] = ext_ref[rows:rows + sub, cs]
        xcb = xc_ref[:, cs].astype(BF16)
        st["r"] = _dot(xcb, wa_ref[g])
        st["i"] = _dot(xcb, wx_ref[g])
        st["y"] = _dot(xn, win_ref[:, ys])
        nlam = -lam_ref[:, cs]
        st["c"] = -LRU_C * (jnp.maximum(nlam, 0.0) + jnp.log1p(jnp.exp(-jnp.abs(nlam))))
        st["h"] = h_ref[:, cs]

    def chain(b):
        rb = slice(b * LRU_RB, (b + 1) * LRU_RB)
        xc = xc_ref[rb, cs]
        r = jax.nn.sigmoid(st["r"][rb] + ba_ref[:, cs])
        i = jax.nn.sigmoid(st["i"][rb] + bx_ref[:, cs])
        log_a = r * st["c"]
        a = jnp.exp(log_a)
        u = jnp.sqrt(-jnp.tanh(log_a) * (a * a + 1.0)) * (i * xc)
        rowmod = lax.broadcasted_iota(jnp.int32, a.shape, 0) & (sub - 1)
        d = 1
        while d < sub:
            keep = rowmod >= d
            a_prev = jnp.where(keep, pltpu.roll(a, d, axis=0), 1.0)
            u_prev = jnp.where(keep, pltpu.roll(u, d, axis=0), 0.0)
            u = a * u_prev + u
            a = a * a_prev
            d *= 2
        h = st["h"]
        outs = []
        for n in range(LRU_RB // sub):
            lo = n * sub
            outs.append(a[lo:lo + sub] * h + u[lo:lo + sub])
            h = a[lo + sub - 1:lo + sub] * h + u[lo + sub - 1:lo + sub]
        st["h"] = h
        hs = jnp.concatenate(outs, axis=0)
        o_ref[rb, cs] = (hs * jax.nn.gelu(st["y"][rb])).astype(BF16)

    def finish():
        h_ref[:, cs] = st["h"]

    blocks = range(rows // LRU_RB)
    return ([project] + [functools.partial(conv, b) for b in blocks] + [gates]
            + [functools.partial(chain, b) for b in blocks] + [finish])


def _rope_kernel(pos_ref, inv_ref, cos_ref, sin_ref):
    ang = pos_ref[...].astype(F32) * inv_ref[...]
    cos_ref[...] = jnp.cos(ang)
    sin_ref[...] = jnp.sin(ang)


def _rope_tables(positions):
    half = RET_QK_DIM // 2
    inv_freq = jnp.power(ROPE_BASE, -jnp.arange(half, dtype=F32) / half).reshape(1, half)
    out = jax.ShapeDtypeStruct((SEQ, half), F32)
    return pl.pallas_call(_rope_kernel, out_shape=(out, out), name="rope_tables")(
        positions.reshape(SEQ, 1), inv_freq)


def _retention_decays():
    log_g = np.log1p(-np.power(2.0, -5.0 - np.arange(RET_HEADS, dtype=np.float64)))
    idx = np.arange(RET_L)
    dist = idx[:, None] - idx[None, :]
    same = (idx[:, None] // CHUNK) == (idx[None, :] // CHUNK)
    earlier = (idx[None, :] // CHUNK) < (idx[:, None] // CHUNK)
    expo = np.where(same, np.abs(dist), dist).astype(np.float64)
    intra = np.where(same | earlier, np.exp(log_g[:, None, None] * expo[None]), 0.0)
    q_decay = np.exp(log_g[:, None] * (idx[None, :] + 1.0))[..., None]
    k_decay = np.exp(log_g[:, None] * (RET_L - 1.0 - idx[None, :]))[..., None]
    block_decay = [float(v) for v in np.exp(log_g * RET_L)]
    return (jnp.asarray(intra, F32), jnp.asarray(q_decay, F32), jnp.asarray(k_decay, F32),
            block_decay)


def _ret_pieces(h, block_decay, xn, win_ref, cos_ref, sin_ref, dm_ref, qd_ref, kd_ref,
                o_ref, state_ref):
    half = RET_QK_DIM // 2
    qc = h * RET_QK_DIM
    vc = 2 * RET_QK + h * RET_V_DIM
    vs = slice(h * RET_V_DIM, (h + 1) * RET_V_DIM)
    st = {}

    def rope(t):
        cos, sin = cos_ref[...], sin_ref[...]
        t1, t2 = t[:, :half], t[:, half:]
        return jnp.concatenate([t1 * cos - t2 * sin, t1 * sin + t2 * cos], axis=-1)

    def project_k():
        st["kr"] = rope(_dot(xn, win_ref[:, RET_QK + qc:RET_QK + qc + RET_QK_DIM])) * (
            RET_QK_DIM ** -0.5)
        st["kb"] = st["kr"].astype(BF16)

    def project_q():
        st["qb"] = rope(_dot(xn, win_ref[:, qc:qc + RET_QK_DIM])).astype(BF16)

    def project_v():
        st["vb"] = _dot(xn, win_ref[:, vc:vc + RET_V_DIM]).astype(BF16)

    def project_g():
        st["g"] = _dot(xn, win_ref[:, RET_V + vc:RET_V + vc + RET_V_DIM])

    def intra(s):
        rs = slice(s * RET_L, (s + 1) * RET_L)
        scores = lax.dot_general(st["qb"][rs], st["kb"][rs], (((1,), (1,)), ((), ())),
                                 preferred_element_type=F32) * dm_ref[h]
        st["intra"] = _dot(scores.astype(BF16), st["vb"][rs])

    def cross(s):
        rs = slice(s * RET_L, (s + 1) * RET_L)
        state = state_ref[h]
        st["out"] = st["intra"] + _dot(st["qb"][rs], state.astype(BF16)) * qd_ref[h]
        kd = (st["kr"][rs] * kd_ref[h]).astype(BF16)
        state_ref[h] = state * block_decay[h] + lax.dot_general(
            kd, st["vb"][rs], (((0,), (0,)), ((), ())), preferred_element_type=F32)

    def emit(s):
        rs = slice(s * RET_L, (s + 1) * RET_L)
        out = st["out"]
        out = out * lax.rsqrt(jnp.mean(out * out, axis=-1, keepdims=True) + EPS)
        o_ref[rs, vs] = (jax.nn.silu(st["g"][rs]) * out).astype(BF16)

    pieces = [project_k, project_q, project_v, project_g]
    for s in range(MIX_TT // RET_L):
        pieces += [functools.partial(f, s) for f in (intra, cross, emit)]
    return pieces


def _interleave(first, second):
    n, m = len(first), len(second)
    order = sorted([((k + 0.5) / n, 0, k) for k in range(n)] + [((k + 0.5) / m, 1, k) for k in range(m)])
    for _, which, k in order:
        (first, second)[which][k]()


def _rec_kernel(block_decay, xn_ref, wl_ref, cw_ref, cb_ref, wa_ref, ba_ref, wx_ref, bx_ref,
                lam_ref, wr_ref, cos_ref, sin_ref, dm_ref, qd_ref, kd_ref,
                za_ref, zb_ref, ext_ref, xc_ref, h_ref, state_ref):
    @pl.when(pl.program_id(1) == 0)
    def _():
        ext_ref[0:V7X_SUBLANES, :] = jnp.zeros((V7X_SUBLANES, LRU_WIDTH), F32)
        h_ref[...] = jnp.zeros_like(h_ref)
        state_ref[...] = jnp.zeros_like(state_ref)

    xn = xn_ref[...]
    assert RET_HEADS == LRU_WIDTH // V7X_MXU_DIM
    for n in range(RET_HEADS):
        _interleave(
            _ret_pieces(n, block_decay, xn, wr_ref, cos_ref, sin_ref, dm_ref, qd_ref, kd_ref,
                        zb_ref, state_ref),
            _lru_pieces(n, xn, wl_ref, cw_ref, cb_ref, wa_ref, ba_ref, wx_ref, bx_ref, lam_ref,
                        za_ref, ext_ref, xc_ref, h_ref))


def _recurrent(xn, layer, w_lru_in, cw, cb, wa, ba, wx, bx, lam, w_ret_in, cos, sin):
    intra, q_decay, k_decay, block_decay = _retention_decays()
    whole3 = lambda b, t: (0, 0, 0)
    rot = pl.BlockSpec((MIX_TT, RET_QK_DIM // 2), lambda b, t: (t, 0))
    vec = _layer_block((1, LRU_WIDTH), layer)
    gw = _layer_block((LRU_WIDTH // V7X_MXU_DIM, V7X_MXU_DIM, V7X_MXU_DIM), layer, **RESIDENT)
    return pl.pallas_call(
        functools.partial(_rec_kernel, block_decay),
        grid=(BATCH, _NT),
        in_specs=[
            _tile(D_MODEL), _layer_block((D_MODEL, LRU_IN), layer, **RESIDENT),
            _layer_block((LRU_CONV, LRU_WIDTH), layer), vec, gw, vec, gw, vec, vec,
            _layer_block((D_MODEL, RET_IN), layer, **RESIDENT), rot, rot,
            pl.BlockSpec((RET_HEADS, RET_L, RET_L), whole3, **RESIDENT),
            pl.BlockSpec((RET_HEADS, RET_L, 1), whole3, **RESIDENT),
            pl.BlockSpec((RET_HEADS, RET_L, 1), whole3, **RESIDENT),
        ],
        out_specs=(_tile(LRU_WIDTH), _tile(RET_V)),
        out_shape=(jax.ShapeDtypeStruct((TOKENS, LRU_WIDTH), BF16),
                   jax.ShapeDtypeStruct((TOKENS, RET_V), BF16)),
        scratch_shapes=[pltpu.VMEM((MIX_TT + V7X_SUBLANES, LRU_WIDTH), F32),
                        pltpu.VMEM((MIX_TT, LRU_WIDTH), F32),
                        pltpu.VMEM((1, LRU_WIDTH), F32),
                        pltpu.VMEM((RET_HEADS, RET_QK_DIM, RET_V_DIM), F32)],
        compiler_params=_params(("parallel", "arbitrary")),
        name="recurrent",
    )(xn, w_lru_in, cw, cb, wa, ba, wx, bx, lam, w_ret_in, cos, sin, intra, q_decay, k_decay)


def _out_kernel(x_ref, xn_ref, za_ref, zb_ref, win_ref, cw_ref, wlo_ref, wro_ref, wso_ref,
                wout_ref, gpost_ref, o_ref, ext_ref):
    rows = x_ref.shape[0]

    @pl.when(pl.program_id(1) == 0)
    def _():
        ext_ref[0:V7X_SUBLANES, :] = jnp.zeros((V7X_SUBLANES, SC_WIDTH), F32)

    xn = xn_ref[...]

    def proj(c):
        return _dot(xn, win_ref[:, c * D_MODEL:(c + 1) * D_MODEL])

    conv = _causal_taps(ext_ref, slice(None), proj(1) * proj(2), cw_ref[...])
    zc = (proj(0) * conv).astype(BF16)
    mix = jax.nn.sigmoid(proj(3)) * _dot(za_ref[...], wlo_ref[...])
    mix = mix + jax.nn.sigmoid(proj(4)) * _dot(zb_ref[...], wro_ref[...])
    mix = mix + jax.nn.sigmoid(proj(5)) * _dot(zc, wso_ref[...])
    h = _dot(mix.astype(BF16), wout_ref[...])
    o_ref[...] = x_ref[...] + _rms(h, gpost_ref[...])


def _mix_out(x, xn, za, zb, layer, w_in, cw, wlo, wro, wso, wout, gpost):
    def weight(k):
        return _layer_block((k, D_MODEL), layer, **RESIDENT)

    return pl.pallas_call(
        _out_kernel,
        grid=(BATCH, _NT),
        in_specs=[
            _tile(D_MODEL), _tile(D_MODEL), _tile(LRU_WIDTH), _tile(RET_V),
            _layer_block((D_MODEL, OUT_IN), layer, **RESIDENT),
            _layer_block((SC_CONV, SC_WIDTH), layer),
            weight(LRU_WIDTH), weight(RET_V), weight(SC_WIDTH), weight(D_MODEL),
            _layer_block((1, D_MODEL), layer),
        ],
        out_specs=_tile(D_MODEL),
        out_shape=jax.ShapeDtypeStruct((TOKENS, D_MODEL), F32),
        scratch_shapes=[pltpu.VMEM((MIX_TT + V7X_SUBLANES, SC_WIDTH), F32)],
        compiler_params=_params(("parallel", "arbitrary")),
        name="mix_out",
    )(x, xn, za, zb, w_in, cw, wlo, wro, wso, wout, gpost)


def _block_diag(w):
    per = V7X_MXU_DIM // LRU_BLOCK_DIM
    groups = LRU_BLOCKS // per
    w5 = w.reshape(DEPTH, groups, per, LRU_BLOCK_DIM, LRU_BLOCK_DIM)
    eye = jnp.eye(per, dtype=w.dtype)
    bd = jnp.einsum("lgaij,ab->lgaibj", w5, eye)
    return bd.reshape(DEPTH, groups, V7X_MXU_DIM, V7X_MXU_DIM).astype(BF16)


def kernel(x, positions, ffn1_pre_g, ffn1_w_in, ffn1_w_out, ffn1_post_g, mix_pre_g, w_mix_in, lru_conv_w, lru_conv_b, lru_w_a, lru_b_a, lru_w_x, lru_b_x, lru_lambda, w_lru_out, w_ret_out, sc_conv_w, w_sc_out, w_mix_out, mix_post_g, ffn2_pre_g, ffn2_w_in, ffn2_w_out, ffn2_post_g):
    vec = lambda v: v.reshape(DEPTH, 1, -1)
    bf = lambda w: w.astype(BF16)
    cos, sin = _rope_tables(positions)
    wa_bd, wx_bd = _block_diag(lru_w_a), _block_diag(lru_w_x)
    f1_in, f1_out, f2_in, f2_out = bf(ffn1_w_in), bf(ffn1_w_out), bf(ffn2_w_in), bf(ffn2_w_out)
    w_lru_in = bf(w_mix_in[:, :, :LRU_IN])
    w_ret_in = bf(w_mix_in[:, :, LRU_IN:LRU_IN + RET_IN])
    w_out_in = bf(w_mix_in[:, :, LRU_IN + RET_IN:])
    wlo, wro, wso, wout = bf(w_lru_out), bf(w_ret_out), bf(w_sc_out), bf(w_mix_out)

    h = x.reshape(TOKENS, D_MODEL)
    for l in range(DEPTH):
        h, hn = _ffn(h, l, vec(ffn1_pre_g), f1_in, f1_out, vec(ffn1_post_g), vec(mix_pre_g))
        za, zb = _recurrent(hn, l, w_lru_in, lru_conv_w, vec(lru_conv_b), wa_bd, vec(lru_b_a),
                            wx_bd, vec(lru_b_x), vec(lru_lambda), w_ret_in, cos, sin)
        h = _mix_out(h, hn, za, zb, l, w_out_in, sc_conv_w, wlo, wro, wso, wout, vec(mix_post_g))
        h = _ffn(h, l, vec(ffn2_pre_g), f2_in, f2_out, vec(ffn2_post_g))
    return h.reshape(BATCH, SEQ, D_MODEL)
```

```python
import functools

import numpy as np
import jax
import jax.numpy as jnp
from jax import lax
from jax.experimental import pallas as pl
from jax.experimental.pallas import tpu as pltpu

D_MODEL = 1024
BATCH = 8
SEQ = 2048
DEPTH = 2
CHUNK = 64
EPS = 1e-6
LRU_WIDTH = D_MODEL
LRU_BLOCKS = 16
LRU_BLOCK_DIM = LRU_WIDTH // LRU_BLOCKS
LRU_CONV = 4
LRU_C = 8.0
RET_QK_DIM = 256
RET_V_DIM = 512
RET_HEADS = D_MODEL // RET_QK_DIM
RET_QK = RET_HEADS * RET_QK_DIM
RET_V = RET_HEADS * RET_V_DIM
ROPE_BASE = 10000.0
SC_WIDTH = D_MODEL
SC_CONV = 3
D_FF = 2816
TOKENS = BATCH * SEQ

LRU_IN = 2 * LRU_WIDTH
RET_IN = 2 * RET_QK + 2 * RET_V
OUT_IN = 3 * SC_WIDTH + 3 * D_MODEL
MIX_IN = LRU_IN + RET_IN + OUT_IN

V7X_MXU_DIM = 256
V7X_SUBLANES = 8
V7X_VMEM_LIMIT_BYTES = 56 * 1024 * 1024

FFN_TM = 1024
FFN_TF = 512
MIX_TT = 512
RET_L = 256
LRU_RB = 64

BF16 = jnp.bfloat16
F32 = jnp.float32
RESIDENT = dict(pipeline_mode=pl.Buffered(1))


def _params(semantics):
    return pltpu.CompilerParams(dimension_semantics=semantics,
                                vmem_limit_bytes=V7X_VMEM_LIMIT_BYTES)


def _rms(x, g):
    return x * lax.rsqrt(jnp.mean(x * x, axis=-1, keepdims=True) + EPS) * g


def _dot(a, b):
    return jnp.dot(a, b, preferred_element_type=F32)


def _layer_block(shape, layer, **kw):
    return pl.BlockSpec((None,) + shape, lambda *_: (layer,) + (0,) * len(shape), **kw)


def _ffn_chunks():
    return [(c, min(FFN_TF, D_FF - c)) for c in range(0, D_FF, FFN_TF)]


def _ffn_kernel(emit_norm, x_ref, gpre_ref, win_ref, wout_ref, gpost_ref, *rest):
    if emit_norm:
        gnext_ref, o_ref, on_ref, xn_ref, acc_ref = rest
    else:
        o_ref, xn_ref, acc_ref = rest
    xn_ref[...] = _rms(x_ref[...], gpre_ref[...]).astype(BF16)
    xn = xn_ref[...]
    for n, (c, w) in enumerate(_ffn_chunks()):
        gate = _dot(xn, win_ref[:, c:c + w])
        up = _dot(xn, win_ref[:, D_FF + c:D_FF + c + w])
        act = (jax.nn.silu(gate) * up).astype(BF16)
        part = _dot(act, wout_ref[c:c + w, :])
        if n == 0:
            acc_ref[...] = part
        else:
            acc_ref[...] += part
    y = x_ref[...] + 0.5 * _rms(acc_ref[...], gpost_ref[...])
    o_ref[...] = y
    if emit_norm:
        on_ref[...] = _rms(y, gnext_ref[...]).astype(BF16)


def _ffn(x, layer, gpre, w_in, w_out, gpost, gnext=None):
    emit_norm = gnext is not None
    rows = pl.BlockSpec((FFN_TM, D_MODEL), lambda i: (i, 0))
    vec = _layer_block((1, D_MODEL), layer)
    in_specs = [rows, vec, _layer_block((D_MODEL, 2 * D_FF), layer, **RESIDENT),
                _layer_block((D_FF, D_MODEL), layer, **RESIDENT), vec]
    args = [x, gpre, w_in, w_out, gpost]
    out_shape = jax.ShapeDtypeStruct((TOKENS, D_MODEL), F32)
    out_specs = rows
    if emit_norm:
        in_specs.append(vec)
        args.append(gnext)
        out_shape = (out_shape, jax.ShapeDtypeStruct((TOKENS, D_MODEL), BF16))
        out_specs = (rows, rows)
    return pl.pallas_call(
        functools.partial(_ffn_kernel, emit_norm),
        grid=(TOKENS // FFN_TM,),
        in_specs=in_specs,
        out_specs=out_specs,
        out_shape=out_shape,
        scratch_shapes=[pltpu.VMEM((FFN_TM, D_MODEL), BF16), pltpu.VMEM((FFN_TM, D_MODEL), F32)],
        compiler_params=_params(("parallel",)),
        name="ffn",
    )(*args)


_NT = SEQ // MIX_TT


def _tile(cols):
    return pl.BlockSpec((MIX_TT, cols), lambda b, t: (b * _NT + t, 0))


def _causal_taps(ext_ref, cs, cur, w):
    taps, rows = w.shape[0], cur.shape[0]
    ext_ref[V7X_SUBLANES:V7X_SUBLANES + rows, cs] = cur
    acc = cur * w[taps - 1:taps]
    for d in range(1, taps):
        acc = acc + ext_ref[V7X_SUBLANES - d:V7X_SUBLANES - d + rows, cs] * w[taps - 1 - d:taps - d]
    ext_ref[0:V7X_SUBLANES, cs] = cur[rows - V7X_SUBLANES:rows]
    return acc


def _lru_pieces(g, xn, win_ref, cw_ref, cb_ref, wa_ref, ba_ref, wx_ref, bx_ref, lam_ref,
                o_ref, ext_ref, xc_ref, h_ref):
    rows = xn.shape[0]
    sub = V7X_SUBLANES
    cs = slice(g * V7X_MXU_DIM, (g + 1) * V7X_MXU_DIM)
    ys = slice(LRU_WIDTH + g * V7X_MXU_DIM, LRU_WIDTH + (g + 1) * V7X_MXU_DIM)
    st = {}

    def project():
        ext_ref[sub:sub + rows, cs] = _dot(xn, win_ref[:, cs])

    def conv(b):
        lo = sub + b * LRU_RB
        w = cw_ref[:, cs]
        acc = ext_ref[lo:lo + LRU_RB, cs] * w[LRU_CONV - 1:LRU_CONV]
        for d in range(1, LRU_CONV):
            acc = acc + ext_ref[lo - d:lo - d + LRU_RB, cs] * w[LRU_CONV - 1 - d:LRU_CONV - d]
        xc_ref[b * LRU_RB:(b + 1) * LRU_RB, cs] = acc + cb_ref[:, cs]

    def gates():
        ext_ref[0:sub, cs] = ext_ref[rows:rows + sub, cs]
        xcb = xc_ref[:, cs].astype(BF16)
        st["r"] = _dot(xcb, wa_ref[g])
        st["i"] = _dot(xcb, wx_ref[g])
        st["y"] = _dot(xn, win_ref[:, ys])
        nlam = -lam_ref[:, cs]
        st["c"] = -LRU_C * (jnp.maximum(nlam, 0.0) + jnp.log1p(jnp.exp(-jnp.abs(nlam))))
        st["h"] = h_ref[:, cs]

    def chain(b):
        rb = slice(b * LRU_RB, (b + 1) * LRU_RB)
        xc = xc_ref[rb, cs]
        r = jax.nn.sigmoid(st["r"][rb] + ba_ref[:, cs])
        i = jax.nn.sigmoid(st["i"][rb] + bx_ref[:, cs])
        log_a = r * st["c"]
        a = jnp.exp(log_a)
        s = -jnp.tanh(log_a) * (a * a + 1.0)
        u = jnp.where(s > 0.0, s * lax.rsqrt(s), 0.0) * (i * xc)
        rowmod = lax.broadcasted_iota(jnp.int32, a.shape, 0) & (sub - 1)
        d = 1
        while d < sub:
            keep = rowmod >= d
            a_prev = jnp.where(keep, pltpu.roll(a, d, axis=0), 1.0)
            u_prev = jnp.where(keep, pltpu.roll(u, d, axis=0), 0.0)
            u = a * u_prev + u
            a = a * a_prev
            d *= 2
        h = st["h"]
        outs = []
        for n in range(LRU_RB // sub):
            lo = n * sub
            outs.append(a[lo:lo + sub] * h + u[lo:lo + sub])
            h = a[lo + sub - 1:lo + sub] * h + u[lo + sub - 1:lo + sub]
        st["h"] = h
        hs = jnp.concatenate(outs, axis=0)
        o_ref[rb, cs] = (hs * jax.nn.gelu(st["y"][rb])).astype(BF16)

    def finish():
        h_ref[:, cs] = st["h"]

    blocks = range(rows // LRU_RB)
    return ([project] + [functools.partial(conv, b) for b in blocks] + [gates]
            + [functools.partial(chain, b) for b in blocks] + [finish])


def _rope_kernel(pos_ref, inv_ref, cos_ref, sin_ref):
    ang = pos_ref[...].astype(F32) * inv_ref[...]
    cos_ref[...] = jnp.cos(ang)
    sin_ref[...] = jnp.sin(ang)


def _rope_tables(positions):
    half = RET_QK_DIM // 2
    inv_freq = jnp.power(ROPE_BASE, -jnp.arange(half, dtype=F32) / half).reshape(1, half)
    out = jax.ShapeDtypeStruct((SEQ, half), F32)
    return pl.pallas_call(_rope_kernel, out_shape=(out, out), name="rope_tables")(
        positions.reshape(SEQ, 1), inv_freq)


def _retention_decays():
    log_g = np.log1p(-np.power(2.0, -5.0 - np.arange(RET_HEADS, dtype=np.float64)))
    idx = np.arange(RET_L)
    dist = idx[:, None] - idx[None, :]
    same = (idx[:, None] // CHUNK) == (idx[None, :] // CHUNK)
    earlier = (idx[None, :] // CHUNK) < (idx[:, None] // CHUNK)
    expo = np.where(same, np.abs(dist), dist).astype(np.float64)
    intra = np.where(same | earlier, np.exp(log_g[:, None, None] * expo[None]), 0.0)
    q_decay = np.exp(log_g[:, None] * (idx[None, :] + 1.0))[..., None]
    k_decay = np.exp(log_g[:, None] * (RET_L - 1.0 - idx[None, :]))[..., None]
    block_decay = [float(v) for v in np.exp(log_g * RET_L)]
    return (jnp.asarray(intra, F32), jnp.asarray(q_decay, F32), jnp.asarray(k_decay, F32),
            block_decay)


def _ret_pieces(h, block_decay, xn, win_ref, cos_ref, sin_ref, dm_ref, qd_ref, kd_ref,
                o_ref, state_ref):
    half = RET_QK_DIM // 2
    qc = h * RET_QK_DIM
    vc = 2 * RET_QK + h * RET_V_DIM
    vs = slice(h * RET_V_DIM, (h + 1) * RET_V_DIM)
    st = {}

    def rope(t):
        cos, sin = cos_ref[...], sin_ref[...]
        t1, t2 = t[:, :half], t[:, half:]
        return jnp.concatenate([t1 * cos - t2 * sin, t1 * sin + t2 * cos], axis=-1)

    def project_k():
        st["kr"] = rope(_dot(xn, win_ref[:, RET_QK + qc:RET_QK + qc + RET_QK_DIM])) * (
            RET_QK_DIM ** -0.5)
        st["kb"] = st["kr"].astype(BF16)

    def project_q():
        st["qb"] = rope(_dot(xn, win_ref[:, qc:qc + RET_QK_DIM])).astype(BF16)

    def project_v():
        st["vb"] = _dot(xn, win_ref[:, vc:vc + RET_V_DIM]).astype(BF16)

    def project_g():
        st["g"] = _dot(xn, win_ref[:, RET_V + vc:RET_V + vc + RET_V_DIM])

    def intra(s):
        rs = slice(s * RET_L, (s + 1) * RET_L)
        scores = lax.dot_general(st["qb"][rs], st["kb"][rs], (((1,), (1,)), ((), ())),
                                 preferred_element_type=F32) * dm_ref[h]
        st["intra"] = _dot(scores.astype(BF16), st["vb"][rs])

    def cross(s):
        rs = slice(s * RET_L, (s + 1) * RET_L)
        state = state_ref[h]
        st["out"] = st["intra"] + _dot(st["qb"][rs], state.astype(BF16)) * qd_ref[h]
        kd = (st["kr"][rs] * kd_ref[h]).astype(BF16)
        state_ref[h] = state * block_decay[h] + lax.dot_general(
            kd, st["vb"][rs], (((0,), (0,)), ((), ())), preferred_element_type=F32)

    def emit(s):
        rs = slice(s * RET_L, (s + 1) * RET_L)
        out = st["out"]
        out = out * lax.rsqrt(jnp.mean(out * out, axis=-1, keepdims=True) + EPS)
        o_ref[rs, vs] = (jax.nn.silu(st["g"][rs]) * out).astype(BF16)

    pieces = [project_k, project_q, project_v, project_g]
    for s in range(MIX_TT // RET_L):
        pieces += [functools.partial(f, s) for f in (intra, cross, emit)]
    return pieces


def _interleave(first, second):
    n, m = len(first), len(second)
    order = sorted([((k + 0.5) / n, 0, k) for k in range(n)] + [((k + 0.5) / m, 1, k) for k in range(m)])
    for _, which, k in order:
        (first, second)[which][k]()


def _ret_kernel(block_decay, xn_ref, wr_ref, cos_ref, sin_ref, dm_ref, qd_ref, kd_ref,
                zb_ref, state_ref):
    @pl.when(pl.program_id(1) == 0)
    def _():
        state_ref[...] = jnp.zeros_like(state_ref)

    xn = xn_ref[...]
    for n in range(RET_HEADS):
        for piece in _ret_pieces(n, block_decay, xn, wr_ref, cos_ref, sin_ref, dm_ref, qd_ref,
                                 kd_ref, zb_ref, state_ref):
            piece()


def _retention(xn, layer, w_ret_in, cos, sin):
    intra, q_decay, k_decay, block_decay = _retention_decays()
    whole3 = lambda b, t: (0, 0, 0)
    rot = pl.BlockSpec((MIX_TT, RET_QK_DIM // 2), lambda b, t: (t, 0))
    return pl.pallas_call(
        functools.partial(_ret_kernel, block_decay),
        grid=(BATCH, _NT),
        in_specs=[
            _tile(D_MODEL), _layer_block((D_MODEL, RET_IN), layer, **RESIDENT), rot, rot,
            pl.BlockSpec((RET_HEADS, RET_L, RET_L), whole3, **RESIDENT),
            pl.BlockSpec((RET_HEADS, RET_L, 1), whole3, **RESIDENT),
            pl.BlockSpec((RET_HEADS, RET_L, 1), whole3, **RESIDENT),
        ],
        out_specs=_tile(RET_V),
        out_shape=jax.ShapeDtypeStruct((TOKENS, RET_V), BF16),
        scratch_shapes=[pltpu.VMEM((RET_HEADS, RET_QK_DIM, RET_V_DIM), F32)],
        compiler_params=_params(("parallel", "arbitrary")),
        name="retention",
    )(xn, w_ret_in, cos, sin, intra, q_decay, k_decay)


def _out_kernel(x_ref, xn_ref, zb_ref, wl_ref, lcw_ref, lcb_ref, wa_ref, ba_ref, wx_ref, bx_ref,
                lam_ref, win_ref, cw_ref, wlo_ref, wro_ref, wso_ref, wout_ref, gpost_ref,
                o_ref, ext_ref, lext_ref, xc_ref, h_ref, za_ref):
    @pl.when(pl.program_id(1) == 0)
    def _():
        ext_ref[0:V7X_SUBLANES, :] = jnp.zeros((V7X_SUBLANES, SC_WIDTH), F32)
        lext_ref[0:V7X_SUBLANES, :] = jnp.zeros((V7X_SUBLANES, LRU_WIDTH), F32)
        h_ref[...] = jnp.zeros_like(h_ref)

    xn = xn_ref[...]
    st = {}

    def proj(c):
        return _dot(xn, win_ref[:, c * D_MODEL:(c + 1) * D_MODEL])

    def conv_in():
        st["p"] = proj(1) * proj(2)

    def conv():
        st["zc"] = (proj(0) * _causal_taps(ext_ref, slice(None), st["p"], cw_ref[...])).astype(BF16)

    def branch_c():
        st["c"] = jax.nn.sigmoid(proj(5)) * _dot(st["zc"], wso_ref[...])

    def branch_b():
        st["b"] = jax.nn.sigmoid(proj(4)) * _dot(zb_ref[...], wro_ref[...])

    lru = []
    for g in range(LRU_WIDTH // V7X_MXU_DIM):
        lru += _lru_pieces(g, xn, wl_ref, lcw_ref, lcb_ref, wa_ref, ba_ref, wx_ref, bx_ref,
                           lam_ref, za_ref, lext_ref, xc_ref, h_ref)
    _interleave([conv_in, conv, branch_c, branch_b], lru)
    mix = jax.nn.sigmoid(proj(3)) * _dot(za_ref[...], wlo_ref[...]) + st["b"] + st["c"]
    h = _dot(mix.astype(BF16), wout_ref[...])
    o_ref[...] = x_ref[...] + _rms(h, gpost_ref[...])


def _mix_out(x, xn, zb, layer, w_lru_in, lcw, lcb, wa, ba, wx, bx, lam, w_in, cw, wlo, wro, wso,
             wout, gpost):
    def weight(k):
        return _layer_block((k, D_MODEL), layer, **RESIDENT)

    vec = _layer_block((1, LRU_WIDTH), layer)
    gw = _layer_block((LRU_WIDTH // V7X_MXU_DIM, V7X_MXU_DIM, V7X_MXU_DIM), layer, **RESIDENT)
    return pl.pallas_call(
        _out_kernel,
        grid=(BATCH, _NT),
        in_specs=[
            _tile(D_MODEL), _tile(D_MODEL), _tile(RET_V),
            _layer_block((D_MODEL, LRU_IN), layer, **RESIDENT),
            _layer_block((LRU_CONV, LRU_WIDTH), layer), vec, gw, vec, gw, vec, vec,
            _layer_block((D_MODEL, OUT_IN), layer, **RESIDENT),
            _layer_block((SC_CONV, SC_WIDTH), layer),
            weight(LRU_WIDTH), weight(RET_V), weight(SC_WIDTH), weight(D_MODEL),
            _layer_block((1, D_MODEL), layer),
        ],
        out_specs=_tile(D_MODEL),
        out_shape=jax.ShapeDtypeStruct((TOKENS, D_MODEL), F32),
        scratch_shapes=[pltpu.VMEM((MIX_TT + V7X_SUBLANES, SC_WIDTH), F32),
                        pltpu.VMEM((MIX_TT + V7X_SUBLANES, LRU_WIDTH), F32),
                        pltpu.VMEM((MIX_TT, LRU_WIDTH), F32),
                        pltpu.VMEM((1, LRU_WIDTH), F32),
                        pltpu.VMEM((MIX_TT, LRU_WIDTH), BF16)],
        compiler_params=_params(("parallel", "arbitrary")),
        name="mix_out",
    )(x, xn, zb, w_lru_in, lcw, lcb, wa, ba, wx, bx, lam, w_in, cw, wlo, wro, wso, wout, gpost)


def _block_diag(w):
    per = V7X_MXU_DIM // LRU_BLOCK_DIM
    groups = LRU_BLOCKS // per
    w5 = w.reshape(DEPTH, groups, per, LRU_BLOCK_DIM, LRU_BLOCK_DIM)
    eye = jnp.eye(per, dtype=w.dtype)
    bd = jnp.einsum("lgaij,ab->lgaibj", w5, eye)
    return bd.reshape(DEPTH, groups, V7X_MXU_DIM, V7X_MXU_DIM).astype(BF16)


def kernel(x, positions, ffn1_pre_g, ffn1_w_in, ffn1_w_out, ffn1_post_g, mix_pre_g, w_mix_in, lru_conv_w, lru_conv_b, lru_w_a, lru_b_a, lru_w_x, lru_b_x, lru_lambda, w_lru_out, w_ret_out, sc_conv_w, w_sc_out, w_mix_out, mix_post_g, ffn2_pre_g, ffn2_w_in, ffn2_w_out, ffn2_post_g):
    vec = lambda v: v.reshape(DEPTH, 1, -1)
    bf = lambda w: w.astype(BF16)
    cos, sin = _rope_tables(positions)
    wa_bd, wx_bd = _block_diag(lru_w_a), _block_diag(lru_w_x)
    f1_in, f1_out, f2_in, f2_out = bf(ffn1_w_in), bf(ffn1_w_out), bf(ffn2_w_in), bf(ffn2_w_out)
    w_lru_in = bf(w_mix_in[:, :, :LRU_IN])
    w_ret_in = bf(w_mix_in[:, :, LRU_IN:LRU_IN + RET_IN])
    w_out_in = bf(w_mix_in[:, :, LRU_IN + RET_IN:])
    wlo, wro, wso, wout = bf(w_lru_out), bf(w_ret_out), bf(w_sc_out), bf(w_mix_out)

    h = x.reshape(TOKENS, D_MODEL)
    for l in range(DEPTH):
        h, hn = _ffn(h, l, vec(ffn1_pre_g), f1_in, f1_out, vec(ffn1_post_g), vec(mix_pre_g))
        zb = _retention(hn, l, w_ret_in, cos, sin)
        h = _mix_out(h, hn, zb, l, w_lru_in, lru_conv_w, vec(lru_conv_b), wa_bd, vec(lru_b_a),
                     wx_bd, vec(lru_b_x), vec(lru_lambda), w_out_in, sc_conv_w, wlo, wro, wso,
                     wout, vec(mix_post_g))
        h = _ffn(h, l, vec(ffn2_pre_g), f2_in, f2_out, vec(ffn2_post_g))
    return h.reshape(BATCH, SEQ, D_MODEL)
```

```python
import functools

import numpy as np
import jax
import jax.numpy as jnp
from jax import lax
from jax.experimental import pallas as pl
from jax.experimental.pallas import tpu as pltpu

D_MODEL = 1024
BATCH = 8
SEQ = 2048
DEPTH = 2
CHUNK = 64
EPS = 1e-6
LRU_WIDTH = D_MODEL
LRU_BLOCKS = 16
LRU_BLOCK_DIM = LRU_WIDTH // LRU_BLOCKS
LRU_CONV = 4
LRU_C = 8.0
RET_QK_DIM = 256
RET_V_DIM = 512
RET_HEADS = D_MODEL // RET_QK_DIM
RET_QK = RET_HEADS * RET_QK_DIM
RET_V = RET_HEADS * RET_V_DIM
ROPE_BASE = 10000.0
SC_WIDTH = D_MODEL
SC_CONV = 3
D_FF = 2816
TOKENS = BATCH * SEQ

LRU_IN = 2 * LRU_WIDTH
RET_IN = 2 * RET_QK + 2 * RET_V
OUT_IN = 3 * SC_WIDTH + 3 * D_MODEL
MIX_IN = LRU_IN + RET_IN + OUT_IN

V7X_MXU_DIM = 256
V7X_SUBLANES = 8
V7X_VMEM_LIMIT_BYTES = 56 * 1024 * 1024

FFN_TM = 512
FFN_TF = 512
FFN_LOAD = 256
MIX_TT = 512
RET_L = 256
RET_LOAD = 512
LRU_RB = 64

BF16 = jnp.bfloat16
F32 = jnp.float32
RESIDENT = dict(pipeline_mode=pl.Buffered(1))


def _params(semantics):
    return pltpu.CompilerParams(dimension_semantics=semantics,
                                vmem_limit_bytes=V7X_VMEM_LIMIT_BYTES)


def _rms(x, g):
    return x * lax.rsqrt(jnp.mean(x * x, axis=-1, keepdims=True) + EPS) * g


def _dot(a, b):
    return jnp.dot(a, b, preferred_element_type=F32)


def _layer_block(shape, layer, **kw):
    return pl.BlockSpec((None,) + shape, lambda *_: (layer,) + (0,) * len(shape), **kw)


def _interleave(first, second):
    n, m = len(first), len(second)
    order = sorted([((k + 0.5) / n, 0, k) for k in range(n)]
                   + [((k + 0.5) / m, 1, k) for k in range(m)])
    for _, which, k in order:
        (first, second)[which][k]()


def _ffn_chunks():
    return [(c, min(FFN_TF, D_FF - c)) for c in range(0, D_FF, FFN_TF)]


_FFN_NLOAD = D_FF // FFN_LOAD
_FFN_TILES = TOKENS // FFN_TM


def _ffn_kernel(emit_norm, x_ref, gpre_ref, wg_ref, wu_ref, wo_ref, gpost_ref, *rest):
    if emit_norm:
        gnext_ref, o_ref, on_ref, win_ref, wout_ref, xn_ref, acc_ref = rest
    else:
        o_ref, win_ref, wout_ref, xn_ref, acc_ref = rest
    s = pl.program_id(0)

    @pl.when(s < _FFN_NLOAD)
    def _():
        win_ref[s] = wg_ref[...].astype(BF16)
        win_ref[_FFN_NLOAD + s] = wu_ref[...].astype(BF16)
        wout_ref[pl.ds(pl.multiple_of(s * FFN_LOAD, FFN_LOAD), FFN_LOAD), :] = (
            wo_ref[...].astype(BF16))

    @pl.when(s >= _FFN_NLOAD)
    def _():
        xn_ref[...] = _rms(x_ref[...], gpre_ref[...]).astype(BF16)
        xn = xn_ref[...]
        for n, (c, w) in enumerate(_ffn_chunks()):
            acts = []
            for k in range(c // FFN_LOAD, (c + w) // FFN_LOAD):
                gate = _dot(xn, win_ref[k])
                up = _dot(xn, win_ref[_FFN_NLOAD + k])
                acts.append((jax.nn.silu(gate) * up).astype(BF16))
            part = _dot(jnp.concatenate(acts, axis=1), wout_ref[c:c + w, :])
            if n == 0:
                acc_ref[...] = part
            else:
                acc_ref[...] += part
        y = x_ref[...] + 0.5 * _rms(acc_ref[...], gpost_ref[...])
        o_ref[...] = y
        if emit_norm:
            on_ref[...] = _rms(y, gnext_ref[...]).astype(BF16)


def _ffn(x, layer, gpre, w_in, w_out, gpost, gnext=None):
    emit_norm = gnext is not None
    chunk_id = lambda s: jnp.minimum(s, _FFN_NLOAD - 1)
    rows = pl.BlockSpec((FFN_TM, D_MODEL), lambda s: (jnp.maximum(s - _FFN_NLOAD, 0), 0))
    vec = _layer_block((1, D_MODEL), layer)
    in_specs = [
        rows, vec,
        pl.BlockSpec((None, D_MODEL, FFN_LOAD), lambda s: (layer, 0, chunk_id(s))),
        pl.BlockSpec((None, D_MODEL, FFN_LOAD), lambda s: (layer, 0, _FFN_NLOAD + chunk_id(s))),
        pl.BlockSpec((None, FFN_LOAD, D_MODEL), lambda s: (layer, chunk_id(s), 0)),
        vec,
    ]
    args = [x, gpre, w_in, w_in, w_out, gpost]
    out_shape = jax.ShapeDtypeStruct((TOKENS, D_MODEL), F32)
    out_specs = rows
    if emit_norm:
        in_specs.append(vec)
        args.append(gnext)
        out_shape = (out_shape, jax.ShapeDtypeStruct((TOKENS, D_MODEL), BF16))
        out_specs = (rows, rows)
    return pl.pallas_call(
        functools.partial(_ffn_kernel, emit_norm),
        grid=(_FFN_NLOAD + _FFN_TILES,),
        in_specs=in_specs,
        out_specs=out_specs,
        out_shape=out_shape,
        scratch_shapes=[pltpu.VMEM((2 * _FFN_NLOAD, D_MODEL, FFN_LOAD), BF16),
                        pltpu.VMEM((D_FF, D_MODEL), BF16),
                        pltpu.VMEM((FFN_TM, D_MODEL), BF16),
                        pltpu.VMEM((FFN_TM, D_MODEL), F32)],
        compiler_params=_params(("arbitrary",)),
        name="ffn",
    )(*args)


_NT = SEQ // MIX_TT


def _tile(cols):
    return pl.BlockSpec((MIX_TT, cols), lambda b, t: (b * _NT + t, 0))


def _causal_taps(ext_ref, cs, cur, w):
    taps, rows = w.shape[0], cur.shape[0]
    ext_ref[V7X_SUBLANES:V7X_SUBLANES + rows, cs] = cur
    acc = cur * w[taps - 1:taps]
    for d in range(1, taps):
        acc = acc + ext_ref[V7X_SUBLANES - d:V7X_SUBLANES - d + rows, cs] * w[taps - 1 - d:taps - d]
    ext_ref[0:V7X_SUBLANES, cs] = cur[rows - V7X_SUBLANES:rows]
    return acc


def _lru_pieces(g, xn, win_ref, cw_ref, cb_ref, wa_ref, ba_ref, wx_ref, bx_ref, lam_ref,
                o_ref, ext_ref, xc_ref, h_ref):
    rows = xn.shape[0]
    sub = V7X_SUBLANES
    cs = slice(g * V7X_MXU_DIM, (g + 1) * V7X_MXU_DIM)
    ys = slice(LRU_WIDTH + g * V7X_MXU_DIM, LRU_WIDTH + (g + 1) * V7X_MXU_DIM)
    st = {}

    def project():
        ext_ref[sub:sub + rows, cs] = _dot(xn, win_ref[:, cs])

    def conv(b):
        lo = sub + b * LRU_RB
        w = cw_ref[:, cs]
        acc = ext_ref[lo:lo + LRU_RB, cs] * w[LRU_CONV - 1:LRU_CONV]
        for d in range(1, LRU_CONV):
            acc = acc + ext_ref[lo - d:lo - d + LRU_RB, cs] * w[LRU_CONV - 1 - d:LRU_CONV - d]
        xc_ref[b * LRU_RB:(b + 1) * LRU_RB, cs] = acc + cb_ref[:, cs]

    def gates():
        ext_ref[0:sub, cs] = ext_ref[rows:rows + sub, cs]
        xcb = xc_ref[:, cs].astype(BF16)
        st["r"] = _dot(xcb, wa_ref[g])
        st["i"] = _dot(xcb, wx_ref[g])
        st["y"] = _dot(xn, win_ref[:, ys])
        nlam = -lam_ref[:, cs]
        st["c"] = -LRU_C * (jnp.maximum(nlam, 0.0) + jnp.log1p(jnp.exp(-jnp.abs(nlam))))
        st["h"] = h_ref[:, cs]

    def chain(b):
        rb = slice(b * LRU_RB, (b + 1) * LRU_RB)
        xc = xc_ref[rb, cs]
        r = jax.nn.sigmoid(st["r"][rb] + ba_ref[:, cs])
        i = jax.nn.sigmoid(st["i"][rb] + bx_ref[:, cs])
        log_a = r * st["c"]
        a = jnp.exp(log_a)
        s = -jnp.tanh(log_a) * (a * a + 1.0)
        u = jnp.where(s > 0.0, s * lax.rsqrt(s), 0.0) * (i * xc)
        rowmod = lax.broadcasted_iota(jnp.int32, a.shape, 0) & (sub - 1)
        d = 1
        while d < sub:
            keep = rowmod >= d
            a_prev = jnp.where(keep, pltpu.roll(a, d, axis=0), 1.0)
            u_prev = jnp.where(keep, pltpu.roll(u, d, axis=0), 0.0)
            u = a * u_prev + u
            a = a * a_prev
            d *= 2
        h = st["h"]
        outs = []
        for n in range(LRU_RB // sub):
            lo = n * sub
            outs.append(a[lo:lo + sub] * h + u[lo:lo + sub])
            h = a[lo + sub - 1:lo + sub] * h + u[lo + sub - 1:lo + sub]
        st["h"] = h
        hs = jnp.concatenate(outs, axis=0)
        o_ref[rb, cs] = (hs * jax.nn.gelu(st["y"][rb])).astype(BF16)

    def finish():
        h_ref[:, cs] = st["h"]

    blocks = range(rows // LRU_RB)
    return ([project] + [functools.partial(conv, b) for b in blocks] + [gates]
            + [functools.partial(chain, b) for b in blocks] + [finish])


def _rope_kernel(pos_ref, inv_ref, cos_ref, sin_ref):
    ang = pos_ref[...].astype(F32) * inv_ref[...]
    cos_ref[...] = jnp.cos(ang)
    sin_ref[...] = jnp.sin(ang)


def _rope_tables(positions):
    half = RET_QK_DIM // 2
    inv_freq = jnp.power(ROPE_BASE, -jnp.arange(half, dtype=F32) / half).reshape(1, half)
    out = jax.ShapeDtypeStruct((SEQ, half), F32)
    return pl.pallas_call(_rope_kernel, out_shape=(out, out), name="rope_tables")(
        positions.reshape(SEQ, 1), inv_freq)


def _retention_decays():
    log_g = np.log1p(-np.power(2.0, -5.0 - np.arange(RET_HEADS, dtype=np.float64)))
    idx = np.arange(RET_L)
    dist = idx[:, None] - idx[None, :]
    same = (idx[:, None] // CHUNK) == (idx[None, :] // CHUNK)
    earlier = (idx[None, :] // CHUNK) < (idx[:, None] // CHUNK)
    expo = np.where(same, np.abs(dist), dist).astype(np.float64)
    intra = np.where(same | earlier, np.exp(log_g[:, None, None] * expo[None]), 0.0)
    q_decay = np.exp(log_g[:, None] * (idx[None, :] + 1.0))[..., None]
    k_decay = np.exp(log_g[:, None] * (RET_L - 1.0 - idx[None, :]))[..., None]
    block_decay = [float(v) for v in np.exp(log_g * RET_L)]
    return (jnp.asarray(intra, F32), jnp.asarray(q_decay, F32), jnp.asarray(k_decay, F32),
            block_decay)


def _ret_pieces(h, block_decay, xn, w_ref, cos_ref, sin_ref, dm_ref, qd_ref, kd_ref,
                o_ref, state_ref):
    def wcols(lo, width):
        off = lo % RET_LOAD
        return w_ref[lo // RET_LOAD, :, off:off + width]

    half = RET_QK_DIM // 2
    qc = h * RET_QK_DIM
    vc = 2 * RET_QK + h * RET_V_DIM
    vs = slice(h * RET_V_DIM, (h + 1) * RET_V_DIM)
    st = {}

    def rope(t):
        cos, sin = cos_ref[...], sin_ref[...]
        t1, t2 = t[:, :half], t[:, half:]
        return jnp.concatenate([t1 * cos - t2 * sin, t1 * sin + t2 * cos], axis=-1)

    def project_k():
        st["kr"] = rope(_dot(xn, wcols(RET_QK + qc, RET_QK_DIM))) * (RET_QK_DIM ** -0.5)
        st["kb"] = st["kr"].astype(BF16)

    def project_q():
        st["qb"] = rope(_dot(xn, wcols(qc, RET_QK_DIM))).astype(BF16)

    def project_v():
        st["vb"] = _dot(xn, wcols(vc, RET_V_DIM)).astype(BF16)

    def project_g():
        st["g"] = _dot(xn, wcols(RET_V + vc, RET_V_DIM))

    def intra(s):
        rs = slice(s * RET_L, (s + 1) * RET_L)
        scores = lax.dot_general(st["qb"][rs], st["kb"][rs], (((1,), (1,)), ((), ())),
                                 preferred_element_type=F32) * dm_ref[h]
        st["intra"] = _dot(scores.astype(BF16), st["vb"][rs])

    def cross(s):
        rs = slice(s * RET_L, (s + 1) * RET_L)
        state = state_ref[h]
        st["out"] = st["intra"] + _dot(st["qb"][rs], state.astype(BF16)) * qd_ref[h]
        kd = (st["kr"][rs] * kd_ref[h]).astype(BF16)
        state_ref[h] = state * block_decay[h] + lax.dot_general(
            kd, st["vb"][rs], (((0,), (0,)), ((), ())), preferred_element_type=F32)

    def emit(s):
        rs = slice(s * RET_L, (s + 1) * RET_L)
        out = st["out"]
        out = out * lax.rsqrt(jnp.mean(out * out, axis=-1, keepdims=True) + EPS)
        o_ref[rs, vs] = (jax.nn.silu(st["g"][rs]) * out).astype(BF16)

    pieces = [project_k, project_q, project_v, project_g]
    for s in range(MIX_TT // RET_L):
        pieces += [functools.partial(f, s) for f in (intra, cross, emit)]
    return pieces


_RET_NLOAD = RET_IN // RET_LOAD


def _ret_kernel(block_decay, xn_ref, wchunk_ref, cos_ref, sin_ref, dm_ref, qd_ref, kd_ref,
                zb_ref, w_ref, state_ref):
    s = pl.program_id(0)
    i = s - _RET_NLOAD

    @pl.when(s < _RET_NLOAD)
    def _():
        w_ref[s] = wchunk_ref[...].astype(BF16)

    @pl.when(i >= 0)
    def _():
        @pl.when(lax.rem(i, _NT) == 0)
        def _():
            state_ref[...] = jnp.zeros_like(state_ref)

        xn = xn_ref[...]
        for n in range(RET_HEADS):
            for piece in _ret_pieces(n, block_decay, xn, w_ref, cos_ref, sin_ref, dm_ref, qd_ref,
                                     kd_ref, zb_ref, state_ref):
                piece()


def _retention(xn, layer, w_mix_in, cos, sin):
    intra, q_decay, k_decay, block_decay = _retention_decays()
    whole3 = lambda s: (0, 0, 0)
    tile = lambda s: jnp.maximum(s - _RET_NLOAD, 0)
    rows = lambda cols: pl.BlockSpec((MIX_TT, cols), lambda s: (tile(s), 0))
    rot = pl.BlockSpec((MIX_TT, RET_QK_DIM // 2), lambda s: (lax.rem(tile(s), _NT), 0))
    first_chunk = LRU_IN // RET_LOAD
    return pl.pallas_call(
        functools.partial(_ret_kernel, block_decay),
        grid=(_RET_NLOAD + BATCH * _NT,),
        in_specs=[
            rows(D_MODEL),
            pl.BlockSpec((None, D_MODEL, RET_LOAD),
                         lambda s: (layer, 0, first_chunk + jnp.minimum(s, _RET_NLOAD - 1))),
            rot, rot,
            pl.BlockSpec((RET_HEADS, RET_L, RET_L), whole3, **RESIDENT),
            pl.BlockSpec((RET_HEADS, RET_L, 1), whole3, **RESIDENT),
            pl.BlockSpec((RET_HEADS, RET_L, 1), whole3, **RESIDENT),
        ],
        out_specs=rows(RET_V),
        out_shape=jax.ShapeDtypeStruct((TOKENS, RET_V), BF16),
        scratch_shapes=[pltpu.VMEM((_RET_NLOAD, D_MODEL, RET_LOAD), BF16),
                        pltpu.VMEM((RET_HEADS, RET_QK_DIM, RET_V_DIM), F32)],
        compiler_params=_params(("arbitrary",)),
        name="retention",
    )(xn, w_mix_in, cos, sin, intra, q_decay, k_decay)


def _out_kernel(x_ref, xn_ref, zb_ref, wl_ref, lcw_ref, lcb_ref, wa_ref, ba_ref, wx_ref, bx_ref,
                lam_ref, win_ref, cw_ref, wlo_ref, wro_ref, wso_ref, wout_ref, gpost_ref,
                o_ref, ext_ref, lext_ref, xc_ref, h_ref, za_ref):
    @pl.when(pl.program_id(1) == 0)
    def _():
        ext_ref[0:V7X_SUBLANES, :] = jnp.zeros((V7X_SUBLANES, SC_WIDTH), F32)
        lext_ref[0:V7X_SUBLANES, :] = jnp.zeros((V7X_SUBLANES, LRU_WIDTH), F32)
        h_ref[...] = jnp.zeros_like(h_ref)

    xn = xn_ref[...]
    st = {}

    def proj(c):
        return _dot(xn, win_ref[:, c * D_MODEL:(c + 1) * D_MODEL])

    def conv_in():
        st["p"] = proj(1) * proj(2)

    def conv():
        st["zc"] = (proj(0) * _causal_taps(ext_ref, slice(None), st["p"], cw_ref[...])).astype(BF16)

    def branch_c():
        st["c"] = jax.nn.sigmoid(proj(5)) * _dot(st["zc"], wso_ref[...])

    def branch_b():
        st["b"] = jax.nn.sigmoid(proj(4)) * _dot(zb_ref[...], wro_ref[...])

    lru = []
    for g in range(LRU_WIDTH // V7X_MXU_DIM):
        lru += _lru_pieces(g, xn, wl_ref, lcw_ref, lcb_ref, wa_ref, ba_ref, wx_ref, bx_ref,
                           lam_ref, za_ref, lext_ref, xc_ref, h_ref)
    _interleave([conv_in, conv, branch_c, branch_b], lru)
    mix = jax.nn.sigmoid(proj(3)) * _dot(za_ref[...], wlo_ref[...]) + st["b"] + st["c"]
    h = _dot(mix.astype(BF16), wout_ref[...])
    o_ref[...] = x_ref[...] + _rms(h, gpost_ref[...])


def _mix_out(x, xn, zb, layer, w_lru_in, lcw, lcb, wa, ba, wx, bx, lam, w_in, cw, wlo, wro, wso,
             wout, gpost):
    def weight(k):
        return _layer_block((k, D_MODEL), layer, **RESIDENT)

    vec = _layer_block((1, LRU_WIDTH), layer)
    gw = _layer_block((LRU_WIDTH // V7X_MXU_DIM, V7X_MXU_DIM, V7X_MXU_DIM), layer, **RESIDENT)
    return pl.pallas_call(
        _out_kernel,
        grid=(BATCH, _NT),
        in_specs=[
            _tile(D_MODEL), _tile(D_MODEL), _tile(RET_V),
            _layer_block((D_MODEL, LRU_IN), layer, **RESIDENT),
            _layer_block((LRU_CONV, LRU_WIDTH), layer), vec, gw, vec, gw, vec, vec,
            _layer_block((D_MODEL, OUT_IN), layer, **RESIDENT),
            _layer_block((SC_CONV, SC_WIDTH), layer),
            weight(LRU_WIDTH), weight(RET_V), weight(SC_WIDTH), weight(D_MODEL),
            _layer_block((1, D_MODEL), layer),
        ],
        out_specs=_tile(D_MODEL),
        out_shape=jax.ShapeDtypeStruct((TOKENS, D_MODEL), F32),
        scratch_shapes=[pltpu.VMEM((MIX_TT + V7X_SUBLANES, SC_WIDTH), F32),
                        pltpu.VMEM((MIX_TT + V7X_SUBLANES, LRU_WIDTH), F32),
                        pltpu.VMEM((MIX_TT, LRU_WIDTH), F32),
                        pltpu.VMEM((1, LRU_WIDTH), F32),
                        pltpu.VMEM((MIX_TT, LRU_WIDTH), BF16)],
        compiler_params=_params(("parallel", "arbitrary")),
        name="mix_out",
    )(x, xn, zb, w_lru_in, lcw, lcb, wa, ba, wx, bx, lam, w_in, cw, wlo, wro, wso, wout, gpost)


def _block_diag(w):
    per = V7X_MXU_DIM // LRU_BLOCK_DIM
    groups = LRU_BLOCKS // per
    w5 = w.reshape(DEPTH, groups, per, LRU_BLOCK_DIM, LRU_BLOCK_DIM)
    eye = jnp.eye(per, dtype=w.dtype)
    bd = jnp.einsum("lgaij,ab->lgaibj", w5, eye)
    return bd.reshape(DEPTH, groups, V7X_MXU_DIM, V7X_MXU_DIM).astype(BF16)


def kernel(x, positions, ffn1_pre_g, ffn1_w_in, ffn1_w_out, ffn1_post_g, mix_pre_g, w_mix_in, lru_conv_w, lru_conv_b, lru_w_a, lru_b_a, lru_w_x, lru_b_x, lru_lambda, w_lru_out, w_ret_out, sc_conv_w, w_sc_out, w_mix_out, mix_post_g, ffn2_pre_g, ffn2_w_in, ffn2_w_out, ffn2_post_g):
    vec = lambda v: v.reshape(DEPTH, 1, -1)
    bf = lambda w: w.astype(BF16)
    cos, sin = _rope_tables(positions)
    wa_bd, wx_bd = _block_diag(lru_w_a), _block_diag(lru_w_x)
    w_lru_in = bf(w_mix_in[:, :, :LRU_IN])
    w_out_in = bf(w_mix_in[:, :, LRU_IN + RET_IN:])
    wlo, wro, wso, wout = bf(w_lru_out), bf(w_ret_out), bf(w_sc_out), bf(w_mix_out)

    h = x.reshape(TOKENS, D_MODEL)
    for l in range(DEPTH):
        h, hn = _ffn(h, l, vec(ffn1_pre_g), ffn1_w_in, ffn1_w_out, vec(ffn1_post_g),
                     vec(mix_pre_g))
        zb = _retention(hn, l, w_mix_in, cos, sin)
        h = _mix_out(h, hn, zb, l, w_lru_in, lru_conv_w, vec(lru_conv_b), wa_bd, vec(lru_b_a),
                     wx_bd, vec(lru_b_x), vec(lru_lambda), w_out_in, sc_conv_w, wlo, wro, wso,
                     wout, vec(mix_post_g))
        h = _ffn(h, l, vec(ffn2_pre_g), ffn2_w_in, ffn2_w_out, vec(ffn2_post_g))
    return h.reshape(BATCH, SEQ, D_MODEL)
```

```python
import functools

import numpy as np
import jax
import jax.numpy as jnp
from jax import lax
from jax.experimental import pallas as pl
from jax.experimental.pallas import tpu as pltpu

D_MODEL = 1024
BATCH = 8
SEQ = 2048
DEPTH = 2
CHUNK = 64
EPS = 1e-6
LRU_WIDTH = D_MODEL
LRU_BLOCKS = 16
LRU_BLOCK_DIM = LRU_WIDTH // LRU_BLOCKS
LRU_CONV = 4
LRU_C = 8.0
RET_QK_DIM = 256
RET_V_DIM = 512
RET_HEADS = D_MODEL // RET_QK_DIM
RET_QK = RET_HEADS * RET_QK_DIM
RET_V = RET_HEADS * RET_V_DIM
ROPE_BASE = 10000.0
SC_WIDTH = D_MODEL
SC_CONV = 3
D_FF = 2816
TOKENS = BATCH * SEQ

LRU_IN = 2 * LRU_WIDTH
RET_IN = 2 * RET_QK + 2 * RET_V
OUT_IN = 3 * SC_WIDTH + 3 * D_MODEL
MIX_IN = LRU_IN + RET_IN + OUT_IN

V7X_MXU_DIM = 256
V7X_SUBLANES = 8
V7X_VMEM_LIMIT_BYTES = 60 * 1024 * 1024

FFN_TM = 512
FFN_TF = 512
FFN_LOAD = 256
MIX_TT = 512
RET_L = 256
RET_LOAD = 512
LRU_RB = 64
CAST_COLS = 512

BF16 = jnp.bfloat16
F32 = jnp.float32
RESIDENT = dict(pipeline_mode=pl.Buffered(1))


def _params(semantics):
    return pltpu.CompilerParams(dimension_semantics=semantics,
                                vmem_limit_bytes=V7X_VMEM_LIMIT_BYTES)


def _rms(x, g):
    return x * lax.rsqrt(jnp.mean(x * x, axis=-1, keepdims=True) + EPS) * g


def _dot(a, b):
    return jnp.dot(a, b, preferred_element_type=F32)


def _layer_block(shape, layer, **kw):
    return pl.BlockSpec((None,) + shape, lambda *_: (layer,) + (0,) * len(shape), **kw)


def _interleave(first, second):
    n, m = len(first), len(second)
    order = sorted([((k + 0.5) / n, 0, k) for k in range(n)]
                   + [((k + 0.5) / m, 1, k) for k in range(m)])
    for _, which, k in order:
        (first, second)[which][k]()


def _ffn_chunks():
    return [(c, min(FFN_TF, D_FF - c)) for c in range(0, D_FF, FFN_TF)]


_FFN_NLOAD = D_FF // FFN_LOAD
_FFN_TILES = TOKENS // FFN_TM


def _ffn_kernel(emit_norm, x_ref, gpre_ref, wg_ref, wu_ref, wo_ref, gpost_ref, *rest):
    if emit_norm:
        gnext_ref, o_ref, on_ref, win_ref, wout_ref, xn_ref, acc_ref = rest
    else:
        o_ref, win_ref, wout_ref, xn_ref, acc_ref = rest
    s = pl.program_id(0)

    @pl.when(s < _FFN_NLOAD)
    def _():
        win_ref[s] = wg_ref[...].astype(BF16)
        win_ref[_FFN_NLOAD + s] = wu_ref[...].astype(BF16)
        wout_ref[pl.ds(pl.multiple_of(s * FFN_LOAD, FFN_LOAD), FFN_LOAD), :] = (
            wo_ref[...].astype(BF16))

    @pl.when(s >= _FFN_NLOAD)
    def _():
        xn_ref[...] = _rms(x_ref[...], gpre_ref[...]).astype(BF16)
        xn = xn_ref[...]
        for n, (c, w) in enumerate(_ffn_chunks()):
            acts = []
            for k in range(c // FFN_LOAD, (c + w) // FFN_LOAD):
                gate = _dot(xn, win_ref[k])
                up = _dot(xn, win_ref[_FFN_NLOAD + k])
                acts.append((jax.nn.silu(gate) * up).astype(BF16))
            part = _dot(jnp.concatenate(acts, axis=1), wout_ref[c:c + w, :])
            if n == 0:
                acc_ref[...] = part
            else:
                acc_ref[...] += part
        y = x_ref[...] + 0.5 * _rms(acc_ref[...], gpost_ref[...])
        o_ref[...] = y
        if emit_norm:
            on_ref[...] = _rms(y, gnext_ref[...]).astype(BF16)


def _ffn(x, layer, gpre, w_in, w_out, gpost, gnext=None):
    emit_norm = gnext is not None
    chunk_id = lambda s: jnp.minimum(s, _FFN_NLOAD - 1)
    rows = pl.BlockSpec((FFN_TM, D_MODEL), lambda s: (jnp.maximum(s - _FFN_NLOAD, 0), 0))
    vec = _layer_block((1, D_MODEL), layer)
    in_specs = [
        rows, vec,
        pl.BlockSpec((None, D_MODEL, FFN_LOAD), lambda s: (layer, 0, chunk_id(s))),
        pl.BlockSpec((None, D_MODEL, FFN_LOAD), lambda s: (layer, 0, _FFN_NLOAD + chunk_id(s))),
        pl.BlockSpec((None, FFN_LOAD, D_MODEL), lambda s: (layer, chunk_id(s), 0)),
        vec,
    ]
    args = [x, gpre, w_in, w_in, w_out, gpost]
    out_shape = jax.ShapeDtypeStruct((TOKENS, D_MODEL), F32)
    out_specs = rows
    if emit_norm:
        in_specs.append(vec)
        args.append(gnext)
        out_shape = (out_shape, jax.ShapeDtypeStruct((TOKENS, D_MODEL), BF16))
        out_specs = (rows, rows)
    return pl.pallas_call(
        functools.partial(_ffn_kernel, emit_norm),
        grid=(_FFN_NLOAD + _FFN_TILES,),
        in_specs=in_specs,
        out_specs=out_specs,
        out_shape=out_shape,
        scratch_shapes=[pltpu.VMEM((2 * _FFN_NLOAD, D_MODEL, FFN_LOAD), BF16),
                        pltpu.VMEM((D_FF, D_MODEL), BF16),
                        pltpu.VMEM((FFN_TM, D_MODEL), BF16),
                        pltpu.VMEM((FFN_TM, D_MODEL), F32)],
        compiler_params=_params(("arbitrary",)),
        name="ffn",
    )(*args)


_NT = SEQ // MIX_TT


def _tile(cols):
    return pl.BlockSpec((MIX_TT, cols), lambda b, t: (b * _NT + t, 0))


def _causal_taps(ext_ref, cs, cur, w):
    taps, rows = w.shape[0], cur.shape[0]
    ext_ref[V7X_SUBLANES:V7X_SUBLANES + rows, cs] = cur
    acc = cur * w[taps - 1:taps]
    for d in range(1, taps):
        acc = acc + ext_ref[V7X_SUBLANES - d:V7X_SUBLANES - d + rows, cs] * w[taps - 1 - d:taps - d]
    ext_ref[0:V7X_SUBLANES, cs] = cur[rows - V7X_SUBLANES:rows]
    return acc


def _lru_pieces(g, xn, win_ref, cw_ref, cb_ref, wa_ref, ba_ref, wx_ref, bx_ref, lam_ref,
                o_ref, ext_ref, xc_ref, h_ref):
    rows = xn.shape[0]
    sub = V7X_SUBLANES
    cs = slice(g * V7X_MXU_DIM, (g + 1) * V7X_MXU_DIM)
    ys = slice(LRU_WIDTH + g * V7X_MXU_DIM, LRU_WIDTH + (g + 1) * V7X_MXU_DIM)
    st = {}

    def project():
        ext_ref[sub:sub + rows, cs] = _dot(xn, win_ref[:, cs])

    def conv(b):
        lo = sub + b * LRU_RB
        w = cw_ref[:, cs]
        acc = ext_ref[lo:lo + LRU_RB, cs] * w[LRU_CONV - 1:LRU_CONV]
        for d in range(1, LRU_CONV):
            acc = acc + ext_ref[lo - d:lo - d + LRU_RB, cs] * w[LRU_CONV - 1 - d:LRU_CONV - d]
        xc_ref[b * LRU_RB:(b + 1) * LRU_RB, cs] = acc + cb_ref[:, cs]

    def gates():
        ext_ref[0:sub, cs] = ext_ref[rows:rows + sub, cs]
        xcb = xc_ref[:, cs].astype(BF16)
        st["r"] = _dot(xcb, wa_ref[g])
        st["i"] = _dot(xcb, wx_ref[g])
        st["y"] = _dot(xn, win_ref[:, ys])
        nlam = -lam_ref[:, cs]
        st["c"] = -LRU_C * (jnp.maximum(nlam, 0.0) + jnp.log1p(jnp.exp(-jnp.abs(nlam))))
        st["h"] = h_ref[:, cs]

    def chain(b):
        rb = slice(b * LRU_RB, (b + 1) * LRU_RB)
        xc = xc_ref[rb, cs]
        r = jax.nn.sigmoid(st["r"][rb] + ba_ref[:, cs])
        i = jax.nn.sigmoid(st["i"][rb] + bx_ref[:, cs])
        log_a = r * st["c"]
        a = jnp.exp(log_a)
        s = -jnp.tanh(log_a) * (a * a + 1.0)
        u = jnp.where(s > 0.0, s * lax.rsqrt(s), 0.0) * (i * xc)
        rowmod = lax.broadcasted_iota(jnp.int32, a.shape, 0) & (sub - 1)
        d = 1
        while d < sub:
            keep = rowmod >= d
            a_prev = jnp.where(keep, pltpu.roll(a, d, axis=0), 1.0)
            u_prev = jnp.where(keep, pltpu.roll(u, d, axis=0), 0.0)
            u = a * u_prev + u
            a = a * a_prev
            d *= 2
        h = st["h"]
        outs = []
        for n in range(LRU_RB // sub):
            lo = n * sub
            outs.append(a[lo:lo + sub] * h + u[lo:lo + sub])
            h = a[lo + sub - 1:lo + sub] * h + u[lo + sub - 1:lo + sub]
        st["h"] = h
        hs = jnp.concatenate(outs, axis=0)
        o_ref[rb, cs] = (hs * jax.nn.gelu(st["y"][rb])).astype(BF16)

    def finish():
        h_ref[:, cs] = st["h"]

    blocks = range(rows // LRU_RB)
    return ([project] + [functools.partial(conv, b) for b in blocks] + [gates]
            + [functools.partial(chain, b) for b in blocks] + [finish])


def _rope_kernel(pos_ref, inv_ref, cos_ref, sin_ref):
    ang = pos_ref[...].astype(F32) * inv_ref[...]
    cos_ref[...] = jnp.cos(ang)
    sin_ref[...] = jnp.sin(ang)


def _rope_tables(positions):
    half = RET_QK_DIM // 2
    inv_freq = jnp.power(ROPE_BASE, -jnp.arange(half, dtype=F32) / half).reshape(1, half)
    out = jax.ShapeDtypeStruct((SEQ, half), F32)
    return pl.pallas_call(_rope_kernel, out_shape=(out, out), name="rope_tables")(
        positions.reshape(SEQ, 1), inv_freq)


def _retention_decays():
    log_g = np.log1p(-np.power(2.0, -5.0 - np.arange(RET_HEADS, dtype=np.float64)))
    idx = np.arange(RET_L)
    dist = idx[:, None] - idx[None, :]
    same = (idx[:, None] // CHUNK) == (idx[None, :] // CHUNK)
    earlier = (idx[None, :] // CHUNK) < (idx[:, None] // CHUNK)
    expo = np.where(same, np.abs(dist), dist).astype(np.float64)
    intra = np.where(same | earlier, np.exp(log_g[:, None, None] * expo[None]), 0.0)
    q_decay = np.exp(log_g[:, None] * (idx[None, :] + 1.0))[..., None]
    k_decay = np.exp(log_g[:, None] * (RET_L - 1.0 - idx[None, :]))[..., None]
    block_decay = [float(v) for v in np.exp(log_g * RET_L)]
    return (jnp.asarray(intra, F32), jnp.asarray(q_decay, F32), jnp.asarray(k_decay, F32),
            block_decay)


def _ret_pieces(h, block_decay, xn, w_ref, cos_ref, sin_ref, dm_ref, qd_ref, kd_ref,
                o_ref, state_ref):
    def wcols(lo, width):
        off = lo % RET_LOAD
        return w_ref[lo // RET_LOAD, :, off:off + width]

    half = RET_QK_DIM // 2
    qc = h * RET_QK_DIM
    vc = 2 * RET_QK + h * RET_V_DIM
    vs = slice(h * RET_V_DIM, (h + 1) * RET_V_DIM)
    st = {}

    def rope(t):
        cos, sin = cos_ref[...], sin_ref[...]
        t1, t2 = t[:, :half], t[:, half:]
        return jnp.concatenate([t1 * cos - t2 * sin, t1 * sin + t2 * cos], axis=-1)

    def project_k():
        st["kr"] = rope(_dot(xn, wcols(RET_QK + qc, RET_QK_DIM))) * (RET_QK_DIM ** -0.5)
        st["kb"] = st["kr"].astype(BF16)

    def project_q():
        st["qb"] = rope(_dot(xn, wcols(qc, RET_QK_DIM))).astype(BF16)

    def project_v():
        st["vb"] = _dot(xn, wcols(vc, RET_V_DIM)).astype(BF16)

    def project_g():
        st["g"] = _dot(xn, wcols(RET_V + vc, RET_V_DIM))

    def intra(s):
        rs = slice(s * RET_L, (s + 1) * RET_L)
        scores = lax.dot_general(st["qb"][rs], st["kb"][rs], (((1,), (1,)), ((), ())),
                                 preferred_element_type=F32) * dm_ref[h]
        st["intra"] = _dot(scores.astype(BF16), st["vb"][rs])

    def cross(s):
        rs = slice(s * RET_L, (s + 1) * RET_L)
        state = state_ref[h]
        st["out"] = st["intra"] + _dot(st["qb"][rs], state.astype(BF16)) * qd_ref[h]
        kd = (st["kr"][rs] * kd_ref[h]).astype(BF16)
        state_ref[h] = state * block_decay[h] + lax.dot_general(
            kd, st["vb"][rs], (((0,), (0,)), ((), ())), preferred_element_type=F32)

    def emit(s):
        rs = slice(s * RET_L, (s + 1) * RET_L)
        out = st["out"]
        out = out * lax.rsqrt(jnp.mean(out * out, axis=-1, keepdims=True) + EPS)
        o_ref[rs, vs] = (jax.nn.silu(st["g"][rs]) * out).astype(BF16)

    pieces = [project_k, project_q, project_v, project_g]
    for s in range(MIX_TT // RET_L):
        pieces += [functools.partial(f, s) for f in (intra, cross, emit)]
    return pieces


_RET_NLOAD = RET_IN // RET_LOAD


def _ret_kernel(block_decay, xn_ref, wchunk_ref, cos_ref, sin_ref, dm_ref, qd_ref, kd_ref,
                zb_ref, w_ref, state_ref):
    s = pl.program_id(0)
    i = s - _RET_NLOAD

    @pl.when(s < _RET_NLOAD)
    def _():
        w_ref[s] = wchunk_ref[...].astype(BF16)

    @pl.when(i >= 0)
    def _():
        @pl.when(lax.rem(i, _NT) == 0)
        def _():
            state_ref[...] = jnp.zeros_like(state_ref)

        xn = xn_ref[...]
        for n in range(RET_HEADS):
            for piece in _ret_pieces(n, block_decay, xn, w_ref, cos_ref, sin_ref, dm_ref, qd_ref,
                                     kd_ref, zb_ref, state_ref):
                piece()


def _retention(xn, layer, w_mix_in, cos, sin):
    intra, q_decay, k_decay, block_decay = _retention_decays()
    whole3 = lambda s: (0, 0, 0)
    tile = lambda s: jnp.maximum(s - _RET_NLOAD, 0)
    rows = lambda cols: pl.BlockSpec((MIX_TT, cols), lambda s: (tile(s), 0))
    rot = pl.BlockSpec((MIX_TT, RET_QK_DIM // 2), lambda s: (lax.rem(tile(s), _NT), 0))
    first_chunk = LRU_IN // RET_LOAD
    return pl.pallas_call(
        functools.partial(_ret_kernel, block_decay),
        grid=(_RET_NLOAD + BATCH * _NT,),
        in_specs=[
            rows(D_MODEL),
            pl.BlockSpec((None, D_MODEL, RET_LOAD),
                         lambda s: (layer, 0, first_chunk + jnp.minimum(s, _RET_NLOAD - 1))),
            rot, rot,
            pl.BlockSpec((RET_HEADS, RET_L, RET_L), whole3, **RESIDENT),
            pl.BlockSpec((RET_HEADS, RET_L, 1), whole3, **RESIDENT),
            pl.BlockSpec((RET_HEADS, RET_L, 1), whole3, **RESIDENT),
        ],
        out_specs=rows(RET_V),
        out_shape=jax.ShapeDtypeStruct((TOKENS, RET_V), BF16),
        scratch_shapes=[pltpu.VMEM((_RET_NLOAD, D_MODEL, RET_LOAD), BF16),
                        pltpu.VMEM((RET_HEADS, RET_QK_DIM, RET_V_DIM), F32)],
        compiler_params=_params(("arbitrary",)),
        name="retention",
    )(xn, w_mix_in, cos, sin, intra, q_decay, k_decay)


def _out_kernel(x_ref, xn_ref, zb_ref, wl_ref, lcw_ref, lcb_ref, wa_ref, ba_ref, wx_ref, bx_ref,
                lam_ref, win_ref, cw_ref, wlo_ref, wro_ref, wso_ref, wout_ref, gpost_ref,
                o_ref, ext_ref, lext_ref, xc_ref, h_ref, za_ref):
    @pl.when(pl.program_id(1) == 0)
    def _():
        ext_ref[0:V7X_SUBLANES, :] = jnp.zeros((V7X_SUBLANES, SC_WIDTH), F32)
        lext_ref[0:V7X_SUBLANES, :] = jnp.zeros((V7X_SUBLANES, LRU_WIDTH), F32)
        h_ref[...] = jnp.zeros_like(h_ref)

    xn = xn_ref[...]
    st = {}

    def proj(c):
        return _dot(xn, win_ref[:, c * D_MODEL:(c + 1) * D_MODEL])

    def conv_in():
        st["p"] = proj(1) * proj(2)

    def conv():
        st["zc"] = (proj(0) * _causal_taps(ext_ref, slice(None), st["p"], cw_ref[...])).astype(BF16)

    def branch_c():
        st["c"] = jax.nn.sigmoid(proj(5)) * _dot(st["zc"], wso_ref[...])

    def branch_b():
        st["b"] = jax.nn.sigmoid(proj(4)) * _dot(zb_ref[...], wro_ref[...])

    lru = []
    for g in range(LRU_WIDTH // V7X_MXU_DIM):
        lru += _lru_pieces(g, xn, wl_ref, lcw_ref, lcb_ref, wa_ref, ba_ref, wx_ref, bx_ref,
                           lam_ref, za_ref, lext_ref, xc_ref, h_ref)
    _interleave([conv_in, conv, branch_c, branch_b], lru)
    mix = jax.nn.sigmoid(proj(3)) * _dot(za_ref[...], wlo_ref[...]) + st["b"] + st["c"]
    h = _dot(mix.astype(BF16), wout_ref[...])
    o_ref[...] = x_ref[...] + _rms(h, gpost_ref[...])


def _mix_out(x, xn, zb, layer, w_lru_in, lcw, lcb, wa, ba, wx, bx, lam, w_in, cw, wlo, wro, wso,
             wout, gpost):
    def weight(k):
        return _layer_block((k, D_MODEL), layer, **RESIDENT)

    vec = _layer_block((1, LRU_WIDTH), layer)
    gw = _layer_block((LRU_WIDTH // V7X_MXU_DIM, V7X_MXU_DIM, V7X_MXU_DIM), layer, **RESIDENT)
    return pl.pallas_call(
        _out_kernel,
        grid=(BATCH, _NT),
        in_specs=[
            _tile(D_MODEL), _tile(D_MODEL), _tile(RET_V),
            _layer_block((D_MODEL, LRU_IN), layer, **RESIDENT),
            _layer_block((LRU_CONV, LRU_WIDTH), layer), vec, gw, vec, gw, vec, vec,
            _layer_block((D_MODEL, OUT_IN), layer, **RESIDENT),
            _layer_block((SC_CONV, SC_WIDTH), layer),
            weight(LRU_WIDTH), weight(RET_V), weight(SC_WIDTH), weight(D_MODEL),
            _layer_block((1, D_MODEL), layer),
        ],
        out_specs=_tile(D_MODEL),
        out_shape=jax.ShapeDtypeStruct((TOKENS, D_MODEL), F32),
        scratch_shapes=[pltpu.VMEM((MIX_TT + V7X_SUBLANES, SC_WIDTH), F32),
                        pltpu.VMEM((MIX_TT + V7X_SUBLANES, LRU_WIDTH), F32),
                        pltpu.VMEM((MIX_TT, LRU_WIDTH), F32),
                        pltpu.VMEM((1, LRU_WIDTH), F32),
                        pltpu.VMEM((MIX_TT, LRU_WIDTH), BF16)],
        compiler_params=_params(("parallel", "arbitrary")),
        name="mix_out",
    )(x, xn, zb, w_lru_in, lcw, lcb, wa, ba, wx, bx, lam, w_in, cw, wlo, wro, wso, wout, gpost)


def _cast_kernel(w_ref, o_ref):
    o_ref[...] = w_ref[...].astype(BF16)


def _to_bf16(w, cols=None):
    depth, rows, width = w.shape
    lo, hi = cols or (0, width)
    assert lo % CAST_COLS == 0 and (hi - lo) % CAST_COLS == 0
    return pl.pallas_call(
        _cast_kernel,
        grid=(depth, (hi - lo) // CAST_COLS),
        in_specs=[pl.BlockSpec((None, rows, CAST_COLS), lambda l, j: (l, 0, lo // CAST_COLS + j))],
        out_specs=pl.BlockSpec((None, rows, CAST_COLS), lambda l, j: (l, 0, j)),
        out_shape=jax.ShapeDtypeStruct((depth, rows, hi - lo), BF16),
        compiler_params=_params(("parallel", "parallel")),
        name="to_bf16",
    )(w)


def _block_diag(w):
    per = V7X_MXU_DIM // LRU_BLOCK_DIM
    groups = LRU_BLOCKS // per
    w5 = w.reshape(DEPTH, groups, per, LRU_BLOCK_DIM, LRU_BLOCK_DIM)
    eye = jnp.eye(per, dtype=w.dtype)
    bd = jnp.einsum("lgaij,ab->lgaibj", w5, eye)
    return bd.reshape(DEPTH, groups, V7X_MXU_DIM, V7X_MXU_DIM).astype(BF16)


def kernel(x, positions, ffn1_pre_g, ffn1_w_in, ffn1_w_out, ffn1_post_g, mix_pre_g, w_mix_in, lru_conv_w, lru_conv_b, lru_w_a, lru_b_a, lru_w_x, lru_b_x, lru_lambda, w_lru_out, w_ret_out, sc_conv_w, w_sc_out, w_mix_out, mix_post_g, ffn2_pre_g, ffn2_w_in, ffn2_w_out, ffn2_post_g):
    vec = lambda v: v.reshape(DEPTH, 1, -1)
    bf = lambda w: w.astype(BF16)
    cos, sin = _rope_tables(positions)
    wa_bd, wx_bd = _block_diag(lru_w_a), _block_diag(lru_w_x)
    w_lru_in = _to_bf16(w_mix_in, (0, LRU_IN))
    w_out_in = _to_bf16(w_mix_in, (LRU_IN + RET_IN, MIX_IN))
    wlo, wro, wso, wout = (_to_bf16(w) for w in (w_lru_out, w_ret_out, w_sc_out, w_mix_out))

    h = x.reshape(TOKENS, D_MODEL)
    for l in range(DEPTH):
        h, hn = _ffn(h, l, vec(ffn1_pre_g), ffn1_w_in, ffn1_w_out, vec(ffn1_post_g),
                     vec(mix_pre_g))
        zb = _retention(hn, l, w_mix_in, cos, sin)
        h = _mix_out(h, hn, zb, l, w_lru_in, lru_conv_w, vec(lru_conv_b), wa_bd, vec(lru_b_a),
                     wx_bd, vec(lru_b_x), vec(lru_lambda), w_out_in, sc_conv_w, wlo, wro, wso,
                     wout, vec(mix_post_g))
        h = _ffn(h, l, vec(ffn2_pre_g), ffn2_w_in, ffn2_w_out, vec(ffn2_post_g))
    return h.reshape(BATCH, SEQ, D_MODEL)
```

```python
import functools

import numpy as np
import jax
import jax.numpy as jnp
from jax import lax
from jax.experimental import pallas as pl
from jax.experimental.pallas import tpu as pltpu

D_MODEL = 1024
BATCH = 8
SEQ = 2048
DEPTH = 2
CHUNK = 64
EPS = 1e-6
LRU_WIDTH = D_MODEL
LRU_BLOCKS = 16
LRU_BLOCK_DIM = LRU_WIDTH // LRU_BLOCKS
LRU_CONV = 4
LRU_C = 8.0
RET_QK_DIM = 256
RET_V_DIM = 512
RET_HEADS = D_MODEL // RET_QK_DIM
RET_QK = RET_HEADS * RET_QK_DIM
RET_V = RET_HEADS * RET_V_DIM
ROPE_BASE = 10000.0
SC_WIDTH = D_MODEL
SC_CONV = 3
D_FF = 2816
TOKENS = BATCH * SEQ

LRU_IN = 2 * LRU_WIDTH
RET_IN = 2 * RET_QK + 2 * RET_V
OUT_IN = 3 * SC_WIDTH + 3 * D_MODEL
MIX_IN = LRU_IN + RET_IN + OUT_IN

V7X_MXU_DIM = 256
V7X_SUBLANES = 8
V7X_LANES = 128
V7X_VMEM_LIMIT_BYTES = 60 * 1024 * 1024

FFN_TM = 512
FFN_TF = 512
FFN_LOAD = 256
MIX_TT = 512
RET_L = 256
RET_LOAD = 512
LRU_RB = 64
CAST_COLS = 512

BF16 = jnp.bfloat16
F32 = jnp.float32
RESIDENT = dict(pipeline_mode=pl.Buffered(1))


def _params(semantics):
    return pltpu.CompilerParams(dimension_semantics=semantics,
                                vmem_limit_bytes=V7X_VMEM_LIMIT_BYTES)


def _rms(x, g):
    return x * lax.rsqrt(jnp.mean(x * x, axis=-1, keepdims=True) + EPS) * g


def _dot(a, b):
    return jnp.dot(a, b, preferred_element_type=F32)


def _layer_block(shape, layer, **kw):
    return pl.BlockSpec((None,) + shape, lambda *_: (layer,) + (0,) * len(shape), **kw)


def _interleave(first, second):
    n, m = len(first), len(second)
    order = sorted([((k + 0.5) / n, 0, k) for k in range(n)]
                   + [((k + 0.5) / m, 1, k) for k in range(m)])
    for _, which, k in order:
        (first, second)[which][k]()


def _ffn_chunks():
    return [(c, min(FFN_TF, D_FF - c)) for c in range(0, D_FF, FFN_TF)]


_FFN_NLOAD = D_FF // FFN_LOAD
_FFN_TILES = TOKENS // FFN_TM


def _ffn_kernel(emit_norm, x_ref, gpre_ref, wg_ref, wu_ref, wo_ref, gpost_ref, *rest):
    if emit_norm:
        gnext_ref, o_ref, on_ref, win_ref, wout_ref, xn_ref, acc_ref = rest
    else:
        o_ref, win_ref, wout_ref, xn_ref, acc_ref = rest
    s = pl.program_id(0)

    @pl.when(s < _FFN_NLOAD)
    def _():
        win_ref[s] = wg_ref[...].astype(BF16)
        win_ref[_FFN_NLOAD + s] = wu_ref[...].astype(BF16)
        wout_ref[pl.ds(pl.multiple_of(s * FFN_LOAD, FFN_LOAD), FFN_LOAD), :] = (
            wo_ref[...].astype(BF16))

    @pl.when(s >= _FFN_NLOAD)
    def _():
        xn_ref[...] = _rms(x_ref[...], gpre_ref[...]).astype(BF16)
        xn = xn_ref[...]
        for n, (c, w) in enumerate(_ffn_chunks()):
            acts = []
            for k in range(c // FFN_LOAD, (c + w) // FFN_LOAD):
                gate = _dot(xn, win_ref[k])
                up = _dot(xn, win_ref[_FFN_NLOAD + k])
                acts.append((jax.nn.silu(gate) * up).astype(BF16))
            part = _dot(jnp.concatenate(acts, axis=1), wout_ref[c:c + w, :])
            if n == 0:
                acc_ref[...] = part
            else:
                acc_ref[...] += part
        y = x_ref[...] + 0.5 * _rms(acc_ref[...], gpost_ref[...])
        o_ref[...] = y
        if emit_norm:
            on_ref[...] = _rms(y, gnext_ref[...]).astype(BF16)


def _ffn(x, layer, gpre, w_in, w_out, gpost, gnext=None):
    emit_norm = gnext is not None
    chunk_id = lambda s: jnp.minimum(s, _FFN_NLOAD - 1)
    rows = pl.BlockSpec((FFN_TM, D_MODEL), lambda s: (jnp.maximum(s - _FFN_NLOAD, 0), 0))
    vec = _layer_block((1, D_MODEL), layer)
    in_specs = [
        rows, vec,
        pl.BlockSpec((None, D_MODEL, FFN_LOAD), lambda s: (layer, 0, chunk_id(s))),
        pl.BlockSpec((None, D_MODEL, FFN_LOAD), lambda s: (layer, 0, _FFN_NLOAD + chunk_id(s))),
        pl.BlockSpec((None, FFN_LOAD, D_MODEL), lambda s: (layer, chunk_id(s), 0)),
        vec,
    ]
    args = [x, gpre, w_in, w_in, w_out, gpost]
    out_shape = jax.ShapeDtypeStruct((TOKENS, D_MODEL), F32)
    out_specs = rows
    if emit_norm:
        in_specs.append(vec)
        args.append(gnext)
        out_shape = (out_shape, jax.ShapeDtypeStruct((TOKENS, D_MODEL), BF16))
        out_specs = (rows, rows)
    return pl.pallas_call(
        functools.partial(_ffn_kernel, emit_norm),
        grid=(_FFN_NLOAD + _FFN_TILES,),
        in_specs=in_specs,
        out_specs=out_specs,
        out_shape=out_shape,
        scratch_shapes=[pltpu.VMEM((2 * _FFN_NLOAD, D_MODEL, FFN_LOAD), BF16),
                        pltpu.VMEM((D_FF, D_MODEL), BF16),
                        pltpu.VMEM((FFN_TM, D_MODEL), BF16),
                        pltpu.VMEM((FFN_TM, D_MODEL), F32)],
        compiler_params=_params(("arbitrary",)),
        name="ffn",
    )(*args)


_NT = SEQ // MIX_TT


def _tile(cols):
    return pl.BlockSpec((MIX_TT, cols), lambda b, t: (b * _NT + t, 0))


def _causal_taps(ext_ref, cs, cur, w):
    taps, rows = w.shape[0], cur.shape[0]
    ext_ref[V7X_SUBLANES:V7X_SUBLANES + rows, cs] = cur
    acc = cur * w[taps - 1:taps]
    for d in range(1, taps):
        acc = acc + ext_ref[V7X_SUBLANES - d:V7X_SUBLANES - d + rows, cs] * w[taps - 1 - d:taps - d]
    ext_ref[0:V7X_SUBLANES, cs] = cur[rows - V7X_SUBLANES:rows]
    return acc


def _lru_pieces(g, xn, win_ref, cw_ref, cb_ref, wa_ref, ba_ref, wx_ref, bx_ref, lam_ref,
                o_ref, ext_ref, xc_ref, h_ref, scan_ref):
    rows = xn.shape[0]
    sub = V7X_SUBLANES
    cs = slice(g * V7X_MXU_DIM, (g + 1) * V7X_MXU_DIM)
    ys = slice(LRU_WIDTH + g * V7X_MXU_DIM, LRU_WIDTH + (g + 1) * V7X_MXU_DIM)
    st = {}

    def project():
        ext_ref[sub:sub + rows, cs] = _dot(xn, win_ref[:, cs])

    def conv(b):
        lo = sub + b * LRU_RB
        w = cw_ref[:, cs]
        acc = ext_ref[lo:lo + LRU_RB, cs] * w[LRU_CONV - 1:LRU_CONV]
        for d in range(1, LRU_CONV):
            acc = acc + ext_ref[lo - d:lo - d + LRU_RB, cs] * w[LRU_CONV - 1 - d:LRU_CONV - d]
        xc_ref[b * LRU_RB:(b + 1) * LRU_RB, cs] = acc + cb_ref[:, cs]

    def gates():
        ext_ref[0:sub, cs] = ext_ref[rows:rows + sub, cs]
        xcb = xc_ref[:, cs].astype(BF16)
        st["r"] = _dot(xcb, wa_ref[g])
        st["i"] = _dot(xcb, wx_ref[g])
        st["y"] = _dot(xn, win_ref[:, ys])
        nlam = -lam_ref[:, cs]
        st["c"] = -LRU_C * (jnp.maximum(nlam, 0.0) + jnp.log1p(jnp.exp(-jnp.abs(nlam))))
        st["h"] = h_ref[:, cs]

    def chain(b):
        rb = slice(b * LRU_RB, (b + 1) * LRU_RB)
        xc = xc_ref[rb, cs]
        r = jax.nn.sigmoid(st["r"][rb] + ba_ref[:, cs])
        i = jax.nn.sigmoid(st["i"][rb] + bx_ref[:, cs])
        log_a = r * st["c"]
        a = jnp.exp(log_a)
        s = -jnp.tanh(log_a) * (a * a + 1.0)
        u = jnp.where(s > 0.0, s * lax.rsqrt(s), 0.0) * (i * xc)
        groups = LRU_RB // sub
        assert groups == sub
        hs, hn = [], []
        for j in range(V7X_MXU_DIM // V7X_LANES):
            ls = slice(j * V7X_LANES, (j + 1) * V7X_LANES)
            scan_ref[0, j] = a[:, ls]
            scan_ref[1, j] = u[:, ls]
            plane = lambda k, p: scan_ref[k, j, pl.ds(p, groups, stride=sub), :]
            pa = [plane(0, p) for p in range(sub)]
            pu = [plane(1, p) for p in range(sub)]
            for p in range(1, sub):
                pu[p] = pa[p] * pu[p - 1] + pu[p]
                pa[p] = pa[p] * pa[p - 1]
            h = st["h"][:, ls]
            enter = []
            for n in range(groups):
                enter.append(h)
                h = pa[sub - 1][n:n + 1] * h + pu[sub - 1][n:n + 1]
            hn.append(h)
            enter = jnp.concatenate(enter, axis=0)
            for p in range(sub):
                scan_ref[0, j, pl.ds(p, groups, stride=sub), :] = pa[p] * enter + pu[p]
            hs.append(scan_ref[0, j])
        st["h"] = jnp.concatenate(hn, axis=1)
        hs = jnp.concatenate(hs, axis=1)
        o_ref[rb, cs] = (hs * jax.nn.gelu(st["y"][rb])).astype(BF16)

    def finish():
        h_ref[:, cs] = st["h"]

    blocks = range(rows // LRU_RB)
    return ([project] + [functools.partial(conv, b) for b in blocks] + [gates]
            + [functools.partial(chain, b) for b in blocks] + [finish])


def _rope_kernel(pos_ref, inv_ref, cos_ref, sin_ref):
    ang = pos_ref[...].astype(F32) * inv_ref[...]
    cos_ref[...] = jnp.cos(ang)
    sin_ref[...] = jnp.sin(ang)


def _rope_tables(positions):
    half = RET_QK_DIM // 2
    inv_freq = jnp.power(ROPE_BASE, -jnp.arange(half, dtype=F32) / half).reshape(1, half)
    out = jax.ShapeDtypeStruct((SEQ, half), F32)
    return pl.pallas_call(_rope_kernel, out_shape=(out, out), name="rope_tables")(
        positions.reshape(SEQ, 1), inv_freq)


def _retention_decays():
    log_g = np.log1p(-np.power(2.0, -5.0 - np.arange(RET_HEADS, dtype=np.float64)))
    idx = np.arange(RET_L)
    dist = idx[:, None] - idx[None, :]
    same = (idx[:, None] // CHUNK) == (idx[None, :] // CHUNK)
    earlier = (idx[None, :] // CHUNK) < (idx[:, None] // CHUNK)
    expo = np.where(same, np.abs(dist), dist).astype(np.float64)
    intra = np.where(same | earlier, np.exp(log_g[:, None, None] * expo[None]), 0.0)
    q_decay = np.exp(log_g[:, None] * (idx[None, :] + 1.0))[..., None]
    k_decay = np.exp(log_g[:, None] * (RET_L - 1.0 - idx[None, :]))[..., None]
    block_decay = [float(v) for v in np.exp(log_g * RET_L)]
    return (jnp.asarray(intra, F32), jnp.asarray(q_decay, F32), jnp.asarray(k_decay, F32),
            block_decay)


def _ret_pieces(h, block_decay, xn, w_ref, cos_ref, sin_ref, dm_ref, qd_ref, kd_ref,
                o_ref, state_ref):
    def wcols(lo, width):
        off = lo % RET_LOAD
        return w_ref[lo // RET_LOAD, :, off:off + width]

    half = RET_QK_DIM // 2
    qc = h * RET_QK_DIM
    vc = 2 * RET_QK + h * RET_V_DIM
    vs = slice(h * RET_V_DIM, (h + 1) * RET_V_DIM)
    st = {}

    def rope(t):
        cos, sin = cos_ref[...], sin_ref[...]
        t1, t2 = t[:, :half], t[:, half:]
        return jnp.concatenate([t1 * cos - t2 * sin, t1 * sin + t2 * cos], axis=-1)

    def project_k():
        st["kr"] = rope(_dot(xn, wcols(RET_QK + qc, RET_QK_DIM))) * (RET_QK_DIM ** -0.5)
        st["kb"] = st["kr"].astype(BF16)

    def project_q():
        st["qb"] = rope(_dot(xn, wcols(qc, RET_QK_DIM))).astype(BF16)

    def project_v():
        st["vb"] = _dot(xn, wcols(vc, RET_V_DIM)).astype(BF16)

    def project_g():
        st["g"] = _dot(xn, wcols(RET_V + vc, RET_V_DIM))

    def intra(s):
        rs = slice(s * RET_L, (s + 1) * RET_L)
        scores = lax.dot_general(st["qb"][rs], st["kb"][rs], (((1,), (1,)), ((), ())),
                                 preferred_element_type=F32) * dm_ref[h]
        st["intra"] = _dot(scores.astype(BF16), st["vb"][rs])

    def cross(s):
        rs = slice(s * RET_L, (s + 1) * RET_L)
        state = state_ref[h]
        st["out"] = st["intra"] + _dot(st["qb"][rs], state.astype(BF16)) * qd_ref[h]
        kd = (st["kr"][rs] * kd_ref[h]).astype(BF16)
        state_ref[h] = state * block_decay[h] + lax.dot_general(
            kd, st["vb"][rs], (((0,), (0,)), ((), ())), preferred_element_type=F32)

    def emit(s):
        rs = slice(s * RET_L, (s + 1) * RET_L)
        out = st["out"]
        out = out * lax.rsqrt(jnp.mean(out * out, axis=-1, keepdims=True) + EPS)
        o_ref[rs, vs] = (jax.nn.silu(st["g"][rs]) * out).astype(BF16)

    pieces = [project_k, project_q, project_v, project_g]
    assert len(pieces) == RET_PROJ_PIECES
    for s in range(MIX_TT // RET_L):
        pieces += [functools.partial(f, s) for f in (intra, cross, emit)]
    return pieces


_RET_NLOAD = RET_IN // RET_LOAD
RET_PROJ_PIECES = 4


def _ret_kernel(block_decay, xn_ref, wchunk_ref, cos_ref, sin_ref, dm_ref, qd_ref, kd_ref,
                zb_ref, w_ref, state_ref):
    s = pl.program_id(0)
    i = s - _RET_NLOAD

    @pl.when(s < _RET_NLOAD)
    def _():
        w_ref[s] = wchunk_ref[...].astype(BF16)

    @pl.when(i >= 0)
    def _():
        @pl.when(lax.rem(i, _NT) == 0)
        def _():
            state_ref[...] = jnp.zeros_like(state_ref)

        xn = xn_ref[...]
        heads = [_ret_pieces(n, block_decay, xn, w_ref, cos_ref, sin_ref, dm_ref, qd_ref, kd_ref,
                             zb_ref, state_ref) for n in range(RET_HEADS)]
        for piece in heads[0][:RET_PROJ_PIECES]:
            piece()
        for n in range(RET_HEADS):
            following = heads[n + 1][:RET_PROJ_PIECES] if n + 1 < RET_HEADS else []
            _interleave(heads[n][RET_PROJ_PIECES:], following)


def _retention(xn, layer, w_mix_in, cos, sin):
    intra, q_decay, k_decay, block_decay = _retention_decays()
    whole3 = lambda s: (0, 0, 0)
    tile = lambda s: jnp.maximum(s - _RET_NLOAD, 0)
    rows = lambda cols: pl.BlockSpec((MIX_TT, cols), lambda s: (tile(s), 0))
    rot = pl.BlockSpec((MIX_TT, RET_QK_DIM // 2), lambda s: (lax.rem(tile(s), _NT), 0))
    first_chunk = LRU_IN // RET_LOAD
    return pl.pallas_call(
        functools.partial(_ret_kernel, block_decay),
        grid=(_RET_NLOAD + BATCH * _NT,),
        in_specs=[
            rows(D_MODEL),
            pl.BlockSpec((None, D_MODEL, RET_LOAD),
                         lambda s: (layer, 0, first_chunk + jnp.minimum(s, _RET_NLOAD - 1))),
            rot, rot,
            pl.BlockSpec((RET_HEADS, RET_L, RET_L), whole3, **RESIDENT),
            pl.BlockSpec((RET_HEADS, RET_L, 1), whole3, **RESIDENT),
            pl.BlockSpec((RET_HEADS, RET_L, 1), whole3, **RESIDENT),
        ],
        out_specs=rows(RET_V),
        out_shape=jax.ShapeDtypeStruct((TOKENS, RET_V), BF16),
        scratch_shapes=[pltpu.VMEM((_RET_NLOAD, D_MODEL, RET_LOAD), BF16),
                        pltpu.VMEM((RET_HEADS, RET_QK_DIM, RET_V_DIM), F32)],
        compiler_params=_params(("arbitrary",)),
        name="retention",
    )(xn, w_mix_in, cos, sin, intra, q_decay, k_decay)


def _out_kernel(x_ref, xn_ref, zb_ref, wl_ref, lcw_ref, lcb_ref, wa_ref, ba_ref, wx_ref, bx_ref,
                lam_ref, win_ref, cw_ref, wlo_ref, wro_ref, wso_ref, wout_ref, gpost_ref,
                o_ref, ext_ref, lext_ref, xc_ref, h_ref, za_ref, scan_ref):
    @pl.when(pl.program_id(1) == 0)
    def _():
        ext_ref[0:V7X_SUBLANES, :] = jnp.zeros((V7X_SUBLANES, SC_WIDTH), F32)
        lext_ref[0:V7X_SUBLANES, :] = jnp.zeros((V7X_SUBLANES, LRU_WIDTH), F32)
        h_ref[...] = jnp.zeros_like(h_ref)

    xn = xn_ref[...]
    st = {}

    def proj(c):
        return _dot(xn, win_ref[:, c * D_MODEL:(c + 1) * D_MODEL])

    def conv_in():
        st["p"] = proj(1) * proj(2)

    def conv():
        st["zc"] = (proj(0) * _causal_taps(ext_ref, slice(None), st["p"], cw_ref[...])).astype(BF16)

    def branch_c():
        st["c"] = jax.nn.sigmoid(proj(5)) * _dot(st["zc"], wso_ref[...])

    def branch_b():
        st["b"] = jax.nn.sigmoid(proj(4)) * _dot(zb_ref[...], wro_ref[...])

    lru = []
    for g in range(LRU_WIDTH // V7X_MXU_DIM):
        lru += _lru_pieces(g, xn, wl_ref, lcw_ref, lcb_ref, wa_ref, ba_ref, wx_ref, bx_ref,
                           lam_ref, za_ref, lext_ref, xc_ref, h_ref, scan_ref)
    _interleave([conv_in, conv, branch_c, branch_b], lru)
    mix = jax.nn.sigmoid(proj(3)) * _dot(za_ref[...], wlo_ref[...]) + st["b"] + st["c"]
    h = _dot(mix.astype(BF16), wout_ref[...])
    o_ref[...] = x_ref[...] + _rms(h, gpost_ref[...])


def _mix_out(x, xn, zb, layer, w_lru_in, lcw, lcb, wa, ba, wx, bx, lam, w_in, cw, wlo, wro, wso,
             wout, gpost):
    def weight(k):
        return _layer_block((k, D_MODEL), layer, **RESIDENT)

    vec = _layer_block((1, LRU_WIDTH), layer)
    gw = _layer_block((LRU_WIDTH // V7X_MXU_DIM, V7X_MXU_DIM, V7X_MXU_DIM), layer, **RESIDENT)
    return pl.pallas_call(
        _out_kernel,
        grid=(BATCH, _NT),
        in_specs=[
            _tile(D_MODEL), _tile(D_MODEL), _tile(RET_V),
            _layer_block((D_MODEL, LRU_IN), layer, **RESIDENT),
            _layer_block((LRU_CONV, LRU_WIDTH), layer), vec, gw, vec, gw, vec, vec,
            _layer_block((D_MODEL, OUT_IN), layer, **RESIDENT),
            _layer_block((SC_CONV, SC_WIDTH), layer),
            weight(LRU_WIDTH), weight(RET_V), weight(SC_WIDTH), weight(D_MODEL),
            _layer_block((1, D_MODEL), layer),
        ],
        out_specs=_tile(D_MODEL),
        out_shape=jax.ShapeDtypeStruct((TOKENS, D_MODEL), F32),
        scratch_shapes=[pltpu.VMEM((MIX_TT + V7X_SUBLANES, SC_WIDTH), F32),
                        pltpu.VMEM((MIX_TT + V7X_SUBLANES, LRU_WIDTH), F32),
                        pltpu.VMEM((MIX_TT, LRU_WIDTH), F32),
                        pltpu.VMEM((1, LRU_WIDTH), F32),
                        pltpu.VMEM((MIX_TT, LRU_WIDTH), BF16),
                        pltpu.VMEM((2, V7X_MXU_DIM // V7X_LANES, LRU_RB, V7X_LANES), F32)],
        compiler_params=_params(("parallel", "arbitrary")),
        name="mix_out",
    )(x, xn, zb, w_lru_in, lcw, lcb, wa, ba, wx, bx, lam, w_in, cw, wlo, wro, wso, wout, gpost)


def _cast_kernel(w_ref, o_ref):
    o_ref[...] = w_ref[...].astype(BF16)


def _to_bf16(w, cols=None):
    depth, rows, width = w.shape
    lo, hi = cols or (0, width)
    assert lo % CAST_COLS == 0 and (hi - lo) % CAST_COLS == 0
    return pl.pallas_call(
        _cast_kernel,
        grid=(depth, (hi - lo) // CAST_COLS),
        in_specs=[pl.BlockSpec((None, rows, CAST_COLS), lambda l, j: (l, 0, lo // CAST_COLS + j))],
        out_specs=pl.BlockSpec((None, rows, CAST_COLS), lambda l, j: (l, 0, j)),
        out_shape=jax.ShapeDtypeStruct((depth, rows, hi - lo), BF16),
        compiler_params=_params(("parallel", "parallel")),
        name="to_bf16",
    )(w)


def _block_diag(w):
    per = V7X_MXU_DIM // LRU_BLOCK_DIM
    groups = LRU_BLOCKS // per
    w5 = w.reshape(DEPTH, groups, per, LRU_BLOCK_DIM, LRU_BLOCK_DIM)
    eye = jnp.eye(per, dtype=w.dtype)
    bd = jnp.einsum("lgaij,ab->lgaibj", w5, eye)
    return bd.reshape(DEPTH, groups, V7X_MXU_DIM, V7X_MXU_DIM).astype(BF16)


def kernel(x, positions, ffn1_pre_g, ffn1_w_in, ffn1_w_out, ffn1_post_g, mix_pre_g, w_mix_in, lru_conv_w, lru_conv_b, lru_w_a, lru_b_a, lru_w_x, lru_b_x, lru_lambda, w_lru_out, w_ret_out, sc_conv_w, w_sc_out, w_mix_out, mix_post_g, ffn2_pre_g, ffn2_w_in, ffn2_w_out, ffn2_post_g):
    vec = lambda v: v.reshape(DEPTH, 1, -1)
    cos, sin = _rope_tables(positions)
    wa_bd, wx_bd = _block_diag(lru_w_a), _block_diag(lru_w_x)
    w_lru_in = _to_bf16(w_mix_in, (0, LRU_IN))
    w_out_in = _to_bf16(w_mix_in, (LRU_IN + RET_IN, MIX_IN))
    wlo, wro, wso, wout = (_to_bf16(w) for w in (w_lru_out, w_ret_out, w_sc_out, w_mix_out))

    h = x.reshape(TOKENS, D_MODEL)
    for l in range(DEPTH):
        h, hn = _ffn(h, l, vec(ffn1_pre_g), ffn1_w_in, ffn1_w_out, vec(ffn1_post_g),
                     vec(mix_pre_g))
        zb = _retention(hn, l, w_mix_in, cos, sin)
        h = _mix_out(h, hn, zb, l, w_lru_in, lru_conv_w, vec(lru_conv_b), wa_bd, vec(lru_b_a),
                     wx_bd, vec(lru_b_x), vec(lru_lambda), w_out_in, sc_conv_w, wlo, wro, wso,
                     wout, vec(mix_post_g))
        h = _ffn(h, l, vec(ffn2_pre_g), ffn2_w_in, ffn2_w_out, vec(ffn2_post_g))
    return h.reshape(BATCH, SEQ, D_MODEL)
```

```python
import functools

import numpy as np
import jax
import jax.numpy as jnp
from jax import lax
from jax.experimental import pallas as pl
from jax.experimental.pallas import tpu as pltpu

D_MODEL = 1024
BATCH = 8
SEQ = 2048
DEPTH = 2
CHUNK = 64
EPS = 1e-6
LRU_WIDTH = D_MODEL
LRU_BLOCKS = 16
LRU_BLOCK_DIM = LRU_WIDTH // LRU_BLOCKS
LRU_CONV = 4
LRU_C = 8.0
RET_QK_DIM = 256
RET_V_DIM = 512
RET_HEADS = D_MODEL // RET_QK_DIM
RET_QK = RET_HEADS * RET_QK_DIM
RET_V = RET_HEADS * RET_V_DIM
ROPE_BASE = 10000.0
SC_WIDTH = D_MODEL
SC_CONV = 3
D_FF = 2816
TOKENS = BATCH * SEQ

LRU_IN = 2 * LRU_WIDTH
RET_IN = 2 * RET_QK + 2 * RET_V
OUT_IN = 3 * SC_WIDTH + 3 * D_MODEL
MIX_IN = LRU_IN + RET_IN + OUT_IN

V7X_MXU_DIM = 256
V7X_SUBLANES = 8
V7X_VMEM_LIMIT_BYTES = 60 * 1024 * 1024

FFN_TM = 1024
FFN_SUB = 256
FFN_RB = 64
FFN_TF = 512
FFN_LOAD = 256
MIX_TT = 512
OUT_TT = 512
RET_L = 256
RET_LOAD = 512
LRU_RB = 64
CAST_COLS = 1024

BF16 = jnp.bfloat16
F32 = jnp.float32
RESIDENT = dict(pipeline_mode=pl.Buffered(1))


def _params(semantics):
    return pltpu.CompilerParams(dimension_semantics=semantics,
                                vmem_limit_bytes=V7X_VMEM_LIMIT_BYTES)


def _rms(x, g):
    return x * lax.rsqrt(jnp.mean(x * x, axis=-1, keepdims=True) + EPS) * g


def _dot(a, b):
    return jnp.dot(a, b, preferred_element_type=F32)


def _layer_block(shape, layer, **kw):
    return pl.BlockSpec((None,) + shape, lambda *_: (layer,) + (0,) * len(shape), **kw)


def _interleave(first, second):
    n, m = len(first), len(second)
    order = sorted([((k + 0.5) / n, 0, k) for k in range(n)]
                   + [((k + 0.5) / m, 1, k) for k in range(m)])
    for _, which, k in order:
        (first, second)[which][k]()


def _ffn_chunks():
    return [(c, min(FFN_TF, D_FF - c)) for c in range(0, D_FF, FFN_TF)]


_FFN_NLOAD = D_FF // FFN_LOAD
_FFN_TILES = TOKENS // FFN_TM


def _ffn_kernel(emit_norm, x_ref, gpre_ref, wg_ref, wu_ref, wo_ref, gpost_ref, *rest):
    if emit_norm:
        gnext_ref, o_ref, on_ref, win_ref, wout_ref, xn_ref, acc_ref = rest
    else:
        o_ref, win_ref, wout_ref, xn_ref, acc_ref = rest
    s = pl.program_id(0)

    @pl.when(s < _FFN_NLOAD)
    def _():
        win_ref[s] = wg_ref[...].astype(BF16)
        win_ref[_FFN_NLOAD + s] = wu_ref[...].astype(BF16)
        wout_ref[pl.ds(pl.multiple_of(s * FFN_LOAD, FFN_LOAD), FFN_LOAD), :] = (
            wo_ref[...].astype(BF16))

    def norm_in(rb):
        xn_ref[rb, :] = _rms(x_ref[rb, :], gpre_ref[...]).astype(BF16)

    def chunk(rs, n, c, w):
        xn = xn_ref[rs, :]
        acts = []
        for k in range(c // FFN_LOAD, (c + w) // FFN_LOAD):
            gate = _dot(xn, win_ref[k])
            up = _dot(xn, win_ref[_FFN_NLOAD + k])
            acts.append((jax.nn.silu(gate) * up).astype(BF16))
        part = _dot(jnp.concatenate(acts, axis=1), wout_ref[c:c + w, :])
        if n == 0:
            acc_ref[rs, :] = part
        else:
            acc_ref[rs, :] += part

    def finish(rb):
        y = x_ref[rb, :] + 0.5 * _rms(acc_ref[rb, :], gpost_ref[...])
        o_ref[rb, :] = y
        if emit_norm:
            on_ref[rb, :] = _rms(y, gnext_ref[...]).astype(BF16)

    @pl.when(s >= _FFN_NLOAD)
    def _():
        subs = FFN_TM // FFN_SUB
        blocks = lambda q: [slice(q * FFN_SUB + b * FFN_RB, q * FFN_SUB + (b + 1) * FFN_RB)
                            for b in range(FFN_SUB // FFN_RB)]
        for rb in blocks(0):
            norm_in(rb)
        for q in range(subs):
            rs = slice(q * FFN_SUB, (q + 1) * FFN_SUB)
            vector = [functools.partial(finish, rb) for rb in blocks(q - 1)] if q > 0 else []
            if q + 1 < subs:
                vector += [functools.partial(norm_in, rb) for rb in blocks(q + 1)]
            _interleave([functools.partial(chunk, rs, n, c, w)
                         for n, (c, w) in enumerate(_ffn_chunks())], vector)
        for rb in blocks(subs - 1):
            finish(rb)


def _ffn(x, layer, gpre, w_in, w_out, gpost, gnext=None):
    emit_norm = gnext is not None
    chunk_id = lambda s: jnp.minimum(s, _FFN_NLOAD - 1)
    rows = pl.BlockSpec((FFN_TM, D_MODEL), lambda s: (jnp.maximum(s - _FFN_NLOAD, 0), 0))
    vec = _layer_block((1, D_MODEL), layer)
    in_specs = [
        rows, vec,
        pl.BlockSpec((None, D_MODEL, FFN_LOAD), lambda s: (layer, 0, chunk_id(s))),
        pl.BlockSpec((None, D_MODEL, FFN_LOAD), lambda s: (layer, 0, _FFN_NLOAD + chunk_id(s))),
        pl.BlockSpec((None, FFN_LOAD, D_MODEL), lambda s: (layer, chunk_id(s), 0)),
        vec,
    ]
    args = [x, gpre, w_in, w_in, w_out, gpost]
    out_shape = jax.ShapeDtypeStruct((TOKENS, D_MODEL), F32)
    out_specs = rows
    if emit_norm:
        in_specs.append(vec)
        args.append(gnext)
        out_shape = (out_shape, jax.ShapeDtypeStruct((TOKENS, D_MODEL), BF16))
        out_specs = (rows, rows)
    return pl.pallas_call(
        functools.partial(_ffn_kernel, emit_norm),
        grid=(_FFN_NLOAD + _FFN_TILES,),
        in_specs=in_specs,
        out_specs=out_specs,
        out_shape=out_shape,
        scratch_shapes=[pltpu.VMEM((2 * _FFN_NLOAD, D_MODEL, FFN_LOAD), BF16),
                        pltpu.VMEM((D_FF, D_MODEL), BF16),
                        pltpu.VMEM((FFN_TM, D_MODEL), BF16),
                        pltpu.VMEM((FFN_TM, D_MODEL), F32)],
        compiler_params=_params(("arbitrary",)),
        name="ffn",
    )(*args)


_NT = SEQ // MIX_TT
_OUT_NT = SEQ // OUT_TT


def _tile(cols):
    return pl.BlockSpec((OUT_TT, cols), lambda b, t: (b * _OUT_NT + t, 0))


def _causal_taps(ext_ref, cs, cur, w):
    taps, rows = w.shape[0], cur.shape[0]
    ext_ref[V7X_SUBLANES:V7X_SUBLANES + rows, cs] = cur
    acc = cur * w[taps - 1:taps]
    for d in range(1, taps):
        acc = acc + ext_ref[V7X_SUBLANES - d:V7X_SUBLANES - d + rows, cs] * w[taps - 1 - d:taps - d]
    ext_ref[0:V7X_SUBLANES, cs] = cur[rows - V7X_SUBLANES:rows]
    return acc


def _lru_pieces(g, xn, win_ref, cw_ref, cb_ref, wa_ref, ba_ref, wx_ref, bx_ref, lam_ref,
                o_ref, ext_ref, xc_ref, h_ref):
    rows = xn.shape[0]
    sub = V7X_SUBLANES
    cs = slice(g * V7X_MXU_DIM, (g + 1) * V7X_MXU_DIM)
    ys = slice(LRU_WIDTH + g * V7X_MXU_DIM, LRU_WIDTH + (g + 1) * V7X_MXU_DIM)
    st = {}

    def project():
        ext_ref[sub:sub + rows, cs] = _dot(xn, win_ref[:, cs])

    def conv(b):
        lo = sub + b * LRU_RB
        w = cw_ref[:, cs]
        acc = ext_ref[lo:lo + LRU_RB, cs] * w[LRU_CONV - 1:LRU_CONV]
        for d in range(1, LRU_CONV):
            acc = acc + ext_ref[lo - d:lo - d + LRU_RB, cs] * w[LRU_CONV - 1 - d:LRU_CONV - d]
        xc_ref[b * LRU_RB:(b + 1) * LRU_RB, cs] = acc + cb_ref[:, cs]

    def gates():
        ext_ref[0:sub, cs] = ext_ref[rows:rows + sub, cs]
        xcb = xc_ref[:, cs].astype(BF16)
        st["r"] = _dot(xcb, wa_ref[g])
        st["i"] = _dot(xcb, wx_ref[g])
        st["y"] = _dot(xn, win_ref[:, ys])
        nlam = -lam_ref[:, cs]
        st["c"] = -LRU_C * (jnp.maximum(nlam, 0.0) + jnp.log1p(jnp.exp(-jnp.abs(nlam))))
        st["h"] = h_ref[:, cs]

    def chain(b):
        rb = slice(b * LRU_RB, (b + 1) * LRU_RB)
        xc = xc_ref[rb, cs]
        r = jax.nn.sigmoid(st["r"][rb] + ba_ref[:, cs])
        i = jax.nn.sigmoid(st["i"][rb] + bx_ref[:, cs])
        log_a = r * st["c"]
        a = jnp.exp(log_a)
        s = -jnp.tanh(log_a) * (a * a + 1.0)
        u = jnp.where(s > 0.0, s * lax.rsqrt(s), 0.0) * (i * xc)
        rowmod = lax.broadcasted_iota(jnp.int32, a.shape, 0) & (sub - 1)
        d = 1
        while d < sub:
            keep = rowmod >= d
            a_prev = jnp.where(keep, pltpu.roll(a, d, axis=0), 1.0)
            u_prev = jnp.where(keep, pltpu.roll(u, d, axis=0), 0.0)
            u = a * u_prev + u
            a = a * a_prev
            d *= 2
        h = st["h"]
        outs = []
        for n in range(LRU_RB // sub):
            lo = n * sub
            outs.append(a[lo:lo + sub] * h + u[lo:lo + sub])
            h = a[lo + sub - 1:lo + sub] * h + u[lo + sub - 1:lo + sub]
        st["h"] = h
        hs = jnp.concatenate(outs, axis=0)
        o_ref[rb, cs] = (hs * jax.nn.gelu(st["y"][rb])).astype(BF16)

    def finish():
        h_ref[:, cs] = st["h"]

    blocks = range(rows // LRU_RB)
    return ([project] + [functools.partial(conv, b) for b in blocks] + [gates]
            + [functools.partial(chain, b) for b in blocks] + [finish])


def _rope_kernel(pos_ref, inv_ref, cos_ref, sin_ref):
    ang = pos_ref[...].astype(F32) * inv_ref[...]
    cos_ref[...] = jnp.cos(ang)
    sin_ref[...] = jnp.sin(ang)


def _rope_tables(positions):
    half = RET_QK_DIM // 2
    inv_freq = jnp.power(ROPE_BASE, -jnp.arange(half, dtype=F32) / half).reshape(1, half)
    out = jax.ShapeDtypeStruct((SEQ, half), F32)
    return pl.pallas_call(_rope_kernel, out_shape=(out, out), name="rope_tables")(
        positions.reshape(SEQ, 1), inv_freq)


def _retention_decays():
    log_g = np.log1p(-np.power(2.0, -5.0 - np.arange(RET_HEADS, dtype=np.float64)))
    idx = np.arange(RET_L)
    dist = idx[:, None] - idx[None, :]
    same = (idx[:, None] // CHUNK) == (idx[None, :] // CHUNK)
    earlier = (idx[None, :] // CHUNK) < (idx[:, None] // CHUNK)
    expo = np.where(same, np.abs(dist), dist).astype(np.float64)
    intra = np.where(same | earlier, np.exp(log_g[:, None, None] * expo[None]), 0.0)
    q_decay = np.exp(log_g[:, None] * (idx[None, :] + 1.0))[..., None]
    k_decay = np.exp(log_g[:, None] * (RET_L - 1.0 - idx[None, :]))[..., None]
    block_decay = [float(v) for v in np.exp(log_g * RET_L)]
    return (jnp.asarray(intra, F32), jnp.asarray(q_decay, F32), jnp.asarray(k_decay, F32),
            block_decay)


def _ret_pieces(h, block_decay, xn, w_ref, cos_ref, sin_ref, dm_ref, qd_ref, kd_ref,
                o_ref, state_ref):
    def wcols(lo, width):
        off = lo % RET_LOAD
        return w_ref[lo // RET_LOAD, :, off:off + width]

    half = RET_QK_DIM // 2
    qc = h * RET_QK_DIM
    vc = 2 * RET_QK + h * RET_V_DIM
    vs = slice(h * RET_V_DIM, (h + 1) * RET_V_DIM)
    st = {}

    def rope(t):
        cos, sin = cos_ref[...], sin_ref[...]
        t1, t2 = t[:, :half], t[:, half:]
        return jnp.concatenate([t1 * cos - t2 * sin, t1 * sin + t2 * cos], axis=-1)

    def project_k():
        st["kr"] = rope(_dot(xn, wcols(RET_QK + qc, RET_QK_DIM))) * (RET_QK_DIM ** -0.5)
        st["kb"] = st["kr"].astype(BF16)

    def project_q():
        st["qb"] = rope(_dot(xn, wcols(qc, RET_QK_DIM))).astype(BF16)

    def project_v():
        st["vb"] = _dot(xn, wcols(vc, RET_V_DIM)).astype(BF16)

    def project_g():
        st["g"] = _dot(xn, wcols(RET_V + vc, RET_V_DIM))

    def intra(s):
        rs = slice(s * RET_L, (s + 1) * RET_L)
        scores = lax.dot_general(st["qb"][rs], st["kb"][rs], (((1,), (1,)), ((), ())),
                                 preferred_element_type=F32) * dm_ref[h]
        st["intra"] = _dot(scores.astype(BF16), st["vb"][rs])

    def cross(s):
        rs = slice(s * RET_L, (s + 1) * RET_L)
        state = state_ref[h]
        st["out"] = st["intra"] + _dot(st["qb"][rs], state.astype(BF16)) * qd_ref[h]
        kd = (st["kr"][rs] * kd_ref[h]).astype(BF16)
        state_ref[h] = state * block_decay[h] + lax.dot_general(
            kd, st["vb"][rs], (((0,), (0,)), ((), ())), preferred_element_type=F32)

    def emit(s):
        rs = slice(s * RET_L, (s + 1) * RET_L)
        out = st["out"]
        out = out * lax.rsqrt(jnp.mean(out * out, axis=-1, keepdims=True) + EPS)
        o_ref[rs, vs] = (jax.nn.silu(st["g"][rs]) * out).astype(BF16)

    pieces = [project_k, project_q, project_v, project_g]
    assert len(pieces) == RET_PROJ_PIECES
    for s in range(MIX_TT // RET_L):
        pieces += [functools.partial(f, s) for f in (intra, cross, emit)]
    return pieces


_RET_NLOAD = RET_IN // RET_LOAD
RET_PROJ_PIECES = 4


def _ret_kernel(block_decay, xn_ref, wchunk_ref, cos_ref, sin_ref, dm_ref, qd_ref, kd_ref,
                zb_ref, w_ref, state_ref):
    s = pl.program_id(0)
    i = s - _RET_NLOAD

    @pl.when(s < _RET_NLOAD)
    def _():
        w_ref[s] = wchunk_ref[...].astype(BF16)

    @pl.when(i >= 0)
    def _():
        @pl.when(lax.rem(i, _NT) == 0)
        def _():
            state_ref[...] = jnp.zeros_like(state_ref)

        xn = xn_ref[...]
        heads = [_ret_pieces(n, block_decay, xn, w_ref, cos_ref, sin_ref, dm_ref, qd_ref, kd_ref,
                             zb_ref, state_ref) for n in range(RET_HEADS)]
        for piece in heads[0][:RET_PROJ_PIECES]:
            piece()
        for n in range(RET_HEADS):
            following = heads[n + 1][:RET_PROJ_PIECES] if n + 1 < RET_HEADS else []
            _interleave(heads[n][RET_PROJ_PIECES:], following)


def _retention(xn, layer, w_mix_in, cos, sin):
    intra, q_decay, k_decay, block_decay = _retention_decays()
    whole3 = lambda s: (0, 0, 0)
    tile = lambda s: jnp.maximum(s - _RET_NLOAD, 0)
    rows = lambda cols: pl.BlockSpec((MIX_TT, cols), lambda s: (tile(s), 0))
    rot = pl.BlockSpec((MIX_TT, RET_QK_DIM // 2), lambda s: (lax.rem(tile(s), _NT), 0))
    first_chunk = LRU_IN // RET_LOAD
    return pl.pallas_call(
        functools.partial(_ret_kernel, block_decay),
        grid=(_RET_NLOAD + BATCH * _NT,),
        in_specs=[
            rows(D_MODEL),
            pl.BlockSpec((None, D_MODEL, RET_LOAD),
                         lambda s: (layer, 0, first_chunk + jnp.minimum(s, _RET_NLOAD - 1))),
            rot, rot,
            pl.BlockSpec((RET_HEADS, RET_L, RET_L), whole3, **RESIDENT),
            pl.BlockSpec((RET_HEADS, RET_L, 1), whole3, **RESIDENT),
            pl.BlockSpec((RET_HEADS, RET_L, 1), whole3, **RESIDENT),
        ],
        out_specs=rows(RET_V),
        out_shape=jax.ShapeDtypeStruct((TOKENS, RET_V), BF16),
        scratch_shapes=[pltpu.VMEM((_RET_NLOAD, D_MODEL, RET_LOAD), BF16),
                        pltpu.VMEM((RET_HEADS, RET_QK_DIM, RET_V_DIM), F32)],
        compiler_params=_params(("arbitrary",)),
        name="retention",
    )(xn, w_mix_in, cos, sin, intra, q_decay, k_decay)


def _out_kernel(x_ref, xn_ref, zb_ref, wl_ref, lcw_ref, lcb_ref, wa_ref, ba_ref, wx_ref, bx_ref,
                lam_ref, win_ref, cw_ref, wlo_ref, wro_ref, wso_ref, wout_ref, gpost_ref,
                o_ref, ext_ref, lext_ref, xc_ref, h_ref, za_ref):
    @pl.when(pl.program_id(1) == 0)
    def _():
        ext_ref[0:V7X_SUBLANES, :] = jnp.zeros((V7X_SUBLANES, SC_WIDTH), F32)
        lext_ref[0:V7X_SUBLANES, :] = jnp.zeros((V7X_SUBLANES, LRU_WIDTH), F32)
        h_ref[...] = jnp.zeros_like(h_ref)

    xn = xn_ref[...]
    st = {}

    def proj(c):
        return _dot(xn, win_ref[:, c * D_MODEL:(c + 1) * D_MODEL])

    def conv_in():
        st["p"] = proj(1) * proj(2)

    def conv():
        st["zc"] = (proj(0) * _causal_taps(ext_ref, slice(None), st["p"], cw_ref[...])).astype(BF16)

    def branch_c():
        st["c"] = jax.nn.sigmoid(proj(5)) * _dot(st["zc"], wso_ref[...])

    def branch_b():
        st["b"] = jax.nn.sigmoid(proj(4)) * _dot(zb_ref[...], wro_ref[...])

    lru = []
    for g in range(LRU_WIDTH // V7X_MXU_DIM):
        lru += _lru_pieces(g, xn, wl_ref, lcw_ref, lcb_ref, wa_ref, ba_ref, wx_ref, bx_ref,
                           lam_ref, za_ref, lext_ref, xc_ref, h_ref)
    _interleave([conv_in, conv, branch_c, branch_b], lru)
    mix = jax.nn.sigmoid(proj(3)) * _dot(za_ref[...], wlo_ref[...]) + st["b"] + st["c"]
    h = _dot(mix.astype(BF16), wout_ref[...])
    o_ref[...] = x_ref[...] + _rms(h, gpost_ref[...])


def _mix_out(x, xn, zb, layer, w_lru_in, lcw, lcb, wa, ba, wx, bx, lam, w_in, cw, wlo, wro, wso,
             wout, gpost):
    def weight(k):
        return _layer_block((k, D_MODEL), layer, **RESIDENT)

    vec = _layer_block((1, LRU_WIDTH), layer)
    gw = _layer_block((LRU_WIDTH // V7X_MXU_DIM, V7X_MXU_DIM, V7X_MXU_DIM), layer, **RESIDENT)
    return pl.pallas_call(
        _out_kernel,
        grid=(BATCH, _OUT_NT),
        in_specs=[
            _tile(D_MODEL), _tile(D_MODEL), _tile(RET_V),
            _layer_block((D_MODEL, LRU_IN), layer, **RESIDENT),
            _layer_block((LRU_CONV, LRU_WIDTH), layer), vec, gw, vec, gw, vec, vec,
            _layer_block((D_MODEL, OUT_IN), layer, **RESIDENT),
            _layer_block((SC_CONV, SC_WIDTH), layer),
            weight(LRU_WIDTH), weight(RET_V), weight(SC_WIDTH), weight(D_MODEL),
            _layer_block((1, D_MODEL), layer),
        ],
        out_specs=_tile(D_MODEL),
        out_shape=jax.ShapeDtypeStruct((TOKENS, D_MODEL), F32),
        scratch_shapes=[pltpu.VMEM((OUT_TT + V7X_SUBLANES, SC_WIDTH), F32),
                        pltpu.VMEM((OUT_TT + V7X_SUBLANES, LRU_WIDTH), F32),
                        pltpu.VMEM((OUT_TT, LRU_WIDTH), F32),
                        pltpu.VMEM((1, LRU_WIDTH), F32),
                        pltpu.VMEM((OUT_TT, LRU_WIDTH), BF16)],
        compiler_params=_params(("parallel", "arbitrary")),
        name="mix_out",
    )(x, xn, zb, w_lru_in, lcw, lcb, wa, ba, wx, bx, lam, w_in, cw, wlo, wro, wso, wout, gpost)


def _cast_kernel(w_ref, o_ref):
    o_ref[...] = w_ref[...].astype(BF16)


def _to_bf16(w, cols=None):
    depth, rows, width = w.shape
    lo, hi = cols or (0, width)
    assert lo % CAST_COLS == 0 and (hi - lo) % CAST_COLS == 0
    return pl.pallas_call(
        _cast_kernel,
        grid=(depth, (hi - lo) // CAST_COLS),
        in_specs=[pl.BlockSpec((None, rows, CAST_COLS), lambda l, j: (l, 0, lo // CAST_COLS + j))],
        out_specs=pl.BlockSpec((None, rows, CAST_COLS), lambda l, j: (l, 0, j)),
        out_shape=jax.ShapeDtypeStruct((depth, rows, hi - lo), BF16),
        compiler_params=_params(("parallel", "parallel")),
        name="to_bf16",
    )(w)


def _block_diag(w):
    per = V7X_MXU_DIM // LRU_BLOCK_DIM
    groups = LRU_BLOCKS // per
    w5 = w.reshape(DEPTH, groups, per, LRU_BLOCK_DIM, LRU_BLOCK_DIM)
    eye = jnp.eye(per, dtype=w.dtype)
    bd = jnp.einsum("lgaij,ab->lgaibj", w5, eye)
    return bd.reshape(DEPTH, groups, V7X_MXU_DIM, V7X_MXU_DIM).astype(BF16)


def kernel(x, positions, ffn1_pre_g, ffn1_w_in, ffn1_w_out, ffn1_post_g, mix_pre_g, w_mix_in, lru_conv_w, lru_conv_b, lru_w_a, lru_b_a, lru_w_x, lru_b_x, lru_lambda, w_lru_out, w_ret_out, sc_conv_w, w_sc_out, w_mix_out, mix_post_g, ffn2_pre_g, ffn2_w_in, ffn2_w_out, ffn2_post_g):
    vec = lambda v: v.reshape(DEPTH, 1, -1)
    cos, sin = _rope_tables(positions)
    wa_bd, wx_bd = _block_diag(lru_w_a), _block_diag(lru_w_x)
    w_lru_in = _to_bf16(w_mix_in, (0, LRU_IN))
    w_out_in = _to_bf16(w_mix_in, (LRU_IN + RET_IN, MIX_IN))
    wlo, wro, wso, wout = (_to_bf16(w) for w in (w_lru_out, w_ret_out, w_sc_out, w_mix_out))

    h = x.reshape(TOKENS, D_MODEL)
    for l in range(DEPTH):
        h, hn = _ffn(h, l, vec(ffn1_pre_g), ffn1_w_in, ffn1_w_out, vec(ffn1_post_g),
                     vec(mix_pre_g))
        zb = _retention(hn, l, w_mix_in, cos, sin)
        h = _mix_out(h, hn, zb, l, w_lru_in, lru_conv_w, vec(lru_conv_b), wa_bd, vec(lru_b_a),
                     wx_bd, vec(lru_b_x), vec(lru_lambda), w_out_in, sc_conv_w, wlo, wro, wso,
                     wout, vec(mix_post_g))
        h = _ffn(h, l, vec(ffn2_pre_g), ffn2_w_in, ffn2_w_out, vec(ffn2_post_g))
    return h.reshape(BATCH, SEQ, D_MODEL)
```

```python
import functools

import numpy as np
import jax
import jax.numpy as jnp
from jax import lax
from jax.experimental import pallas as pl
from jax.experimental.pallas import tpu as pltpu

D_MODEL = 1024
BATCH = 8
SEQ = 2048
DEPTH = 2
CHUNK = 64
EPS = 1e-6
LRU_WIDTH = D_MODEL
LRU_BLOCKS = 16
LRU_BLOCK_DIM = LRU_WIDTH // LRU_BLOCKS
LRU_CONV = 4
LRU_C = 8.0
RET_QK_DIM = 256
RET_V_DIM = 512
RET_HEADS = D_MODEL // RET_QK_DIM
RET_QK = RET_HEADS * RET_QK_DIM
RET_V = RET_HEADS * RET_V_DIM
ROPE_BASE = 10000.0
SC_WIDTH = D_MODEL
SC_CONV = 3
D_FF = 2816
TOKENS = BATCH * SEQ

LRU_IN = 2 * LRU_WIDTH
RET_IN = 2 * RET_QK + 2 * RET_V
OUT_IN = 3 * SC_WIDTH + 3 * D_MODEL
MIX_IN = LRU_IN + RET_IN + OUT_IN

V7X_MXU_DIM = 256
V7X_SUBLANES = 8
V7X_VMEM_LIMIT_BYTES = 60 * 1024 * 1024

FFN_TM = 512
FFN_TF = 512
FFN_LOAD = 256
MIX_TT = 512
OUT_TT = 512
RET_L = 256
RET_LOAD = 512
LRU_RB = 64
CAST_COLS = 1024

BF16 = jnp.bfloat16
F32 = jnp.float32
RESIDENT = dict(pipeline_mode=pl.Buffered(1))


def _params(semantics):
    return pltpu.CompilerParams(dimension_semantics=semantics,
                                vmem_limit_bytes=V7X_VMEM_LIMIT_BYTES)


def _rms(x, g):
    return x * lax.rsqrt(jnp.mean(x * x, axis=-1, keepdims=True) + EPS) * g


def _dot(a, b):
    return jnp.dot(a, b, preferred_element_type=F32)


def _layer_block(shape, layer, **kw):
    return pl.BlockSpec((None,) + shape, lambda *_: (layer,) + (0,) * len(shape), **kw)


def _interleave(first, second):
    n, m = len(first), len(second)
    order = sorted([((k + 0.5) / n, 0, k) for k in range(n)]
                   + [((k + 0.5) / m, 1, k) for k in range(m)])
    for _, which, k in order:
        (first, second)[which][k]()


def _ffn_chunks():
    return [(c, min(FFN_TF, D_FF - c)) for c in range(0, D_FF, FFN_TF)]


_FFN_NLOAD = D_FF // FFN_LOAD
_FFN_TILES = TOKENS // FFN_TM


def _ffn_kernel(emit_norm, x_ref, gpre_ref, wg_ref, wu_ref, wo_ref, gpost_ref, *rest):
    if emit_norm:
        gnext_ref, o_ref, on_ref, win_ref, wout_ref, xn_ref, acc_ref = rest
    else:
        o_ref, win_ref, wout_ref, xn_ref, acc_ref = rest
    s = pl.program_id(0)

    @pl.when(s < _FFN_NLOAD)
    def _():
        win_ref[s] = wg_ref[...].astype(BF16)
        win_ref[_FFN_NLOAD + s] = wu_ref[...].astype(BF16)
        wout_ref[pl.ds(pl.multiple_of(s * FFN_LOAD, FFN_LOAD), FFN_LOAD), :] = (
            wo_ref[...].astype(BF16))

    @pl.when(s >= _FFN_NLOAD)
    def _():
        xn_ref[...] = _rms(x_ref[...], gpre_ref[...]).astype(BF16)
        xn = xn_ref[...]
        for n, (c, w) in enumerate(_ffn_chunks()):
            acts = []
            for k in range(c // FFN_LOAD, (c + w) // FFN_LOAD):
                gate = _dot(xn, win_ref[k])
                up = _dot(xn, win_ref[_FFN_NLOAD + k])
                acts.append((jax.nn.silu(gate) * up).astype(BF16))
            part = _dot(jnp.concatenate(acts, axis=1), wout_ref[c:c + w, :])
            if n == 0:
                acc_ref[...] = part
            else:
                acc_ref[...] += part
        y = x_ref[...] + 0.5 * _rms(acc_ref[...], gpost_ref[...])
        o_ref[...] = y
        if emit_norm:
            on_ref[...] = _rms(y, gnext_ref[...]).astype(BF16)


def _ffn(x, layer, gpre, w_in, w_out, gpost, gnext=None):
    emit_norm = gnext is not None
    chunk_id = lambda s: jnp.minimum(s, _FFN_NLOAD - 1)
    rows = pl.BlockSpec((FFN_TM, D_MODEL), lambda s: (jnp.maximum(s - _FFN_NLOAD, 0), 0))
    vec = _layer_block((1, D_MODEL), layer)
    in_specs = [
        rows, vec,
        pl.BlockSpec((None, D_MODEL, FFN_LOAD), lambda s: (layer, 0, chunk_id(s))),
        pl.BlockSpec((None, D_MODEL, FFN_LOAD), lambda s: (layer, 0, _FFN_NLOAD + chunk_id(s))),
        pl.BlockSpec((None, FFN_LOAD, D_MODEL), lambda s: (layer, chunk_id(s), 0)),
        vec,
    ]
    args = [x, gpre, w_in, w_in, w_out, gpost]
    out_shape = jax.ShapeDtypeStruct((TOKENS, D_MODEL), F32)
    out_specs = rows
    if emit_norm:
        in_specs.append(vec)
        args.append(gnext)
        out_shape = (out_shape, jax.ShapeDtypeStruct((TOKENS, D_MODEL), BF16))
        out_specs = (rows, rows)
    return pl.pallas_call(
        functools.partial(_ffn_kernel, emit_norm),
        grid=(_FFN_NLOAD + _FFN_TILES,),
        in_specs=in_specs,
        out_specs=out_specs,
        out_shape=out_shape,
        scratch_shapes=[pltpu.VMEM((2 * _FFN_NLOAD, D_MODEL, FFN_LOAD), BF16),
                        pltpu.VMEM((D_FF, D_MODEL), BF16),
                        pltpu.VMEM((FFN_TM, D_MODEL), BF16),
                        pltpu.VMEM((FFN_TM, D_MODEL), F32)],
        compiler_params=_params(("arbitrary",)),
        name="ffn",
    )(*args)


_NT = SEQ // MIX_TT
_OUT_NT = SEQ // OUT_TT


def _tile(cols):
    return pl.BlockSpec((OUT_TT, cols), lambda b, t: (b * _OUT_NT + t, 0))


def _causal_taps(ext_ref, cs, cur, w):
    taps, rows = w.shape[0], cur.shape[0]
    ext_ref[V7X_SUBLANES:V7X_SUBLANES + rows, cs] = cur
    acc = cur * w[taps - 1:taps]
    for d in range(1, taps):
        acc = acc + ext_ref[V7X_SUBLANES - d:V7X_SUBLANES - d + rows, cs] * w[taps - 1 - d:taps - d]
    ext_ref[0:V7X_SUBLANES, cs] = cur[rows - V7X_SUBLANES:rows]
    return acc


def _lru_pieces(g, xn, win_ref, cw_ref, cb_ref, wa_ref, ba_ref, wx_ref, bx_ref, lam_ref,
                o_ref, ext_ref, xc_ref, h_ref):
    rows = xn.shape[0]
    sub = V7X_SUBLANES
    cs = slice(g * V7X_MXU_DIM, (g + 1) * V7X_MXU_DIM)
    ys = slice(LRU_WIDTH + g * V7X_MXU_DIM, LRU_WIDTH + (g + 1) * V7X_MXU_DIM)
    st = {}

    def project():
        ext_ref[sub:sub + rows, cs] = _dot(xn, win_ref[:, cs])

    def conv(b):
        lo = sub + b * LRU_RB
        w = cw_ref[:, cs]
        acc = ext_ref[lo:lo + LRU_RB, cs] * w[LRU_CONV - 1:LRU_CONV]
        for d in range(1, LRU_CONV):
            acc = acc + ext_ref[lo - d:lo - d + LRU_RB, cs] * w[LRU_CONV - 1 - d:LRU_CONV - d]
        xc_ref[b * LRU_RB:(b + 1) * LRU_RB, cs] = acc + cb_ref[:, cs]

    def gates():
        ext_ref[0:sub, cs] = ext_ref[rows:rows + sub, cs]
        xcb = xc_ref[:, cs].astype(BF16)
        st["r"] = _dot(xcb, wa_ref[g])
        st["i"] = _dot(xcb, wx_ref[g])
        st["y"] = _dot(xn, win_ref[:, ys])
        nlam = -lam_ref[:, cs]
        st["c"] = -LRU_C * (jnp.maximum(nlam, 0.0) + jnp.log1p(jnp.exp(-jnp.abs(nlam))))
        st["h"] = h_ref[:, cs]

    def chain(b):
        rb = slice(b * LRU_RB, (b + 1) * LRU_RB)
        xc = xc_ref[rb, cs]
        r = jax.nn.sigmoid(st["r"][rb] + ba_ref[:, cs])
        i = jax.nn.sigmoid(st["i"][rb] + bx_ref[:, cs])
        log_a = r * st["c"]
        a = jnp.exp(log_a)
        s = -jnp.tanh(log_a) * (a * a + 1.0)
        u = jnp.where(s > 0.0, s * lax.rsqrt(s), 0.0) * (i * xc)
        rowmod = lax.broadcasted_iota(jnp.int32, a.shape, 0) & (sub - 1)
        d = 1
        while d < sub:
            keep = rowmod >= d
            a_prev = jnp.where(keep, pltpu.roll(a, d, axis=0), 1.0)
            u_prev = jnp.where(keep, pltpu.roll(u, d, axis=0), 0.0)
            u = a * u_prev + u
            a = a * a_prev
            d *= 2
        h = st["h"]
        outs = []
        for n in range(LRU_RB // sub):
            lo = n * sub
            outs.append(a[lo:lo + sub] * h + u[lo:lo + sub])
            h = a[lo + sub - 1:lo + sub] * h + u[lo + sub - 1:lo + sub]
        st["h"] = h
        hs = jnp.concatenate(outs, axis=0)
        o_ref[rb, cs] = (hs * jax.nn.gelu(st["y"][rb])).astype(BF16)

    def finish():
        h_ref[:, cs] = st["h"]

    blocks = range(rows // LRU_RB)
    return ([project] + [functools.partial(conv, b) for b in blocks] + [gates]
            + [functools.partial(chain, b) for b in blocks] + [finish])


def _rope_kernel(pos_ref, inv_ref, cos_ref, sin_ref):
    ang = pos_ref[...].astype(F32) * inv_ref[...]
    cos_ref[...] = jnp.cos(ang)
    sin_ref[...] = jnp.sin(ang)


def _rope_tables(positions):
    half = RET_QK_DIM // 2
    inv_freq = jnp.power(ROPE_BASE, -jnp.arange(half, dtype=F32) / half).reshape(1, half)
    out = jax.ShapeDtypeStruct((SEQ, half), F32)
    return pl.pallas_call(_rope_kernel, out_shape=(out, out), name="rope_tables")(
        positions.reshape(SEQ, 1), inv_freq)


def _retention_decays():
    log_g = np.log1p(-np.power(2.0, -5.0 - np.arange(RET_HEADS, dtype=np.float64)))
    idx = np.arange(RET_L)
    dist = idx[:, None] - idx[None, :]
    same = (idx[:, None] // CHUNK) == (idx[None, :] // CHUNK)
    earlier = (idx[None, :] // CHUNK) < (idx[:, None] // CHUNK)
    expo = np.where(same, np.abs(dist), dist).astype(np.float64)
    intra = np.where(same | earlier, np.exp(log_g[:, None, None] * expo[None]), 0.0)
    q_decay = np.exp(log_g[:, None] * (idx[None, :] + 1.0))[..., None]
    k_decay = np.exp(log_g[:, None] * (RET_L - 1.0 - idx[None, :]))[..., None]
    block_decay = [float(v) for v in np.exp(log_g * RET_L)]
    return (jnp.asarray(intra, F32), jnp.asarray(q_decay, F32), jnp.asarray(k_decay, F32),
            block_decay)


def _ret_pieces(h, block_decay, xn, w_ref, cos_ref, sin_ref, dm_ref, qd_ref, kd_ref,
                o_ref, state_ref):
    def wcols(lo, width):
        off = lo % RET_LOAD
        return w_ref[lo // RET_LOAD, :, off:off + width]

    half = RET_QK_DIM // 2
    qc = h * RET_QK_DIM
    vc = 2 * RET_QK + h * RET_V_DIM
    vs = slice(h * RET_V_DIM, (h + 1) * RET_V_DIM)
    st = {}

    def rope(t):
        cos, sin = cos_ref[...], sin_ref[...]
        t1, t2 = t[:, :half], t[:, half:]
        return jnp.concatenate([t1 * cos - t2 * sin, t1 * sin + t2 * cos], axis=-1)

    def project_k():
        st["kr"] = rope(_dot(xn, wcols(RET_QK + qc, RET_QK_DIM))) * (RET_QK_DIM ** -0.5)
        st["kb"] = st["kr"].astype(BF16)

    def project_q():
        st["qb"] = rope(_dot(xn, wcols(qc, RET_QK_DIM))).astype(BF16)

    def project_v():
        st["vb"] = _dot(xn, wcols(vc, RET_V_DIM)).astype(BF16)

    def project_g():
        st["g"] = _dot(xn, wcols(RET_V + vc, RET_V_DIM))

    def intra(s):
        rs = slice(s * RET_L, (s + 1) * RET_L)
        scores = lax.dot_general(st["qb"][rs], st["kb"][rs], (((1,), (1,)), ((), ())),
                                 preferred_element_type=F32) * dm_ref[h]
        st["intra"] = _dot(scores.astype(BF16), st["vb"][rs])

    def cross(s):
        rs = slice(s * RET_L, (s + 1) * RET_L)
        state = state_ref[h]
        st["out"] = st["intra"] + _dot(st["qb"][rs], state.astype(BF16)) * qd_ref[h]
        kd = (st["kr"][rs] * kd_ref[h]).astype(BF16)
        state_ref[h] = state * block_decay[h] + lax.dot_general(
            kd, st["vb"][rs], (((0,), (0,)), ((), ())), preferred_element_type=F32)

    def emit(s):
        rs = slice(s * RET_L, (s + 1) * RET_L)
        out = st["out"]
        out = out * lax.rsqrt(jnp.mean(out * out, axis=-1, keepdims=True) + EPS)
        o_ref[rs, vs] = (jax.nn.silu(st["g"][rs]) * out).astype(BF16)

    pieces = [project_k, project_q, project_v, project_g]
    assert len(pieces) == RET_PROJ_PIECES
    for s in range(MIX_TT // RET_L):
        pieces += [functools.partial(f, s) for f in (intra, cross, emit)]
    return pieces


_RET_NLOAD = RET_IN // RET_LOAD
RET_PROJ_PIECES = 4


def _ret_kernel(block_decay, xn_ref, wchunk_ref, cos_ref, sin_ref, dm_ref, qd_ref, kd_ref,
                zb_ref, w_ref, state_ref):
    s = pl.program_id(0)
    i = s - _RET_NLOAD

    @pl.when(s < _RET_NLOAD)
    def _():
        w_ref[s] = wchunk_ref[...].astype(BF16)

    @pl.when(i >= 0)
    def _():
        @pl.when(lax.rem(i, _NT) == 0)
        def _():
            state_ref[...] = jnp.zeros_like(state_ref)

        xn = xn_ref[...]
        heads = [_ret_pieces(n, block_decay, xn, w_ref, cos_ref, sin_ref, dm_ref, qd_ref, kd_ref,
                             zb_ref, state_ref) for n in range(RET_HEADS)]
        for piece in heads[0][:RET_PROJ_PIECES]:
            piece()
        for n in range(RET_HEADS):
            following = heads[n + 1][:RET_PROJ_PIECES] if n + 1 < RET_HEADS else []
            _interleave(heads[n][RET_PROJ_PIECES:], following)


def _retention(xn, layer, w_mix_in, cos, sin):
    intra, q_decay, k_decay, block_decay = _retention_decays()
    whole3 = lambda s: (0, 0, 0)
    tile = lambda s: jnp.maximum(s - _RET_NLOAD, 0)
    rows = lambda cols: pl.BlockSpec((MIX_TT, cols), lambda s: (tile(s), 0))
    rot = pl.BlockSpec((MIX_TT, RET_QK_DIM // 2), lambda s: (lax.rem(tile(s), _NT), 0))
    first_chunk = LRU_IN // RET_LOAD
    return pl.pallas_call(
        functools.partial(_ret_kernel, block_decay),
        grid=(_RET_NLOAD + BATCH * _NT,),
        in_specs=[
            rows(D_MODEL),
            pl.BlockSpec((None, D_MODEL, RET_LOAD),
                         lambda s: (layer, 0, first_chunk + jnp.minimum(s, _RET_NLOAD - 1))),
            rot, rot,
            pl.BlockSpec((RET_HEADS, RET_L, RET_L), whole3, **RESIDENT),
            pl.BlockSpec((RET_HEADS, RET_L, 1), whole3, **RESIDENT),
            pl.BlockSpec((RET_HEADS, RET_L, 1), whole3, **RESIDENT),
        ],
        out_specs=rows(RET_V),
        out_shape=jax.ShapeDtypeStruct((TOKENS, RET_V), BF16),
        scratch_shapes=[pltpu.VMEM((_RET_NLOAD, D_MODEL, RET_LOAD), BF16),
                        pltpu.VMEM((RET_HEADS, RET_QK_DIM, RET_V_DIM), F32)],
        compiler_params=_params(("arbitrary",)),
        name="retention",
    )(xn, w_mix_in, cos, sin, intra, q_decay, k_decay)


def _out_kernel(x_ref, xn_ref, zb_ref, wl_ref, lcw_ref, lcb_ref, wa_ref, ba_ref, wx_ref, bx_ref,
                lam_ref, win_ref, cw_ref, wlo_ref, wro_ref, wso_ref, wout_ref, gpost_ref,
                o_ref, ext_ref, lext_ref, xc_ref, h_ref, za_ref):
    @pl.when(pl.program_id(1) == 0)
    def _():
        ext_ref[0:V7X_SUBLANES, :] = jnp.zeros((V7X_SUBLANES, SC_WIDTH), F32)
        lext_ref[0:V7X_SUBLANES, :] = jnp.zeros((V7X_SUBLANES, LRU_WIDTH), F32)
        h_ref[...] = jnp.zeros_like(h_ref)

    xn = xn_ref[...]
    st = {}

    def proj(c):
        return _dot(xn, win_ref[:, c * D_MODEL:(c + 1) * D_MODEL])

    def conv_in():
        st["p"] = proj(1) * proj(2)

    def conv():
        st["zc"] = (proj(0) * _causal_taps(ext_ref, slice(None), st["p"], cw_ref[...])).astype(BF16)

    def branch_c():
        st["c"] = jax.nn.sigmoid(proj(5)) * _dot(st["zc"], wso_ref[...])

    def branch_b():
        st["b"] = jax.nn.sigmoid(proj(4)) * _dot(zb_ref[...], wro_ref[...])

    lru = []
    for g in range(LRU_WIDTH // V7X_MXU_DIM):
        lru += _lru_pieces(g, xn, wl_ref, lcw_ref, lcb_ref, wa_ref, ba_ref, wx_ref, bx_ref,
                           lam_ref, za_ref, lext_ref, xc_ref, h_ref)
    _interleave([conv_in, conv, branch_c, branch_b], lru)
    mix = jax.nn.sigmoid(proj(3)) * _dot(za_ref[...], wlo_ref[...]) + st["b"] + st["c"]
    h = _dot(mix.astype(BF16), wout_ref[...])
    o_ref[...] = x_ref[...] + _rms(h, gpost_ref[...])


def _mix_out(x, xn, zb, layer, w_lru_in, lcw, lcb, wa, ba, wx, bx, lam, w_in, cw, wlo, wro, wso,
             wout, gpost):
    def weight(k):
        return _layer_block((k, D_MODEL), layer, **RESIDENT)

    vec = _layer_block((1, LRU_WIDTH), layer)
    gw = _layer_block((LRU_WIDTH // V7X_MXU_DIM, V7X_MXU_DIM, V7X_MXU_DIM), layer, **RESIDENT)
    return pl.pallas_call(
        _out_kernel,
        grid=(BATCH, _OUT_NT),
        in_specs=[
            _tile(D_MODEL), _tile(D_MODEL), _tile(RET_V),
            _layer_block((D_MODEL, LRU_IN), layer, **RESIDENT),
            _layer_block((LRU_CONV, LRU_WIDTH), layer), vec, gw, vec, gw, vec, vec,
            _layer_block((D_MODEL, OUT_IN), layer, **RESIDENT),
            _layer_block((SC_CONV, SC_WIDTH), layer),
            weight(LRU_WIDTH), weight(RET_V), weight(SC_WIDTH), weight(D_MODEL),
            _layer_block((1, D_MODEL), layer),
        ],
        out_specs=_tile(D_MODEL),
        out_shape=jax.ShapeDtypeStruct((TOKENS, D_MODEL), F32),
        scratch_shapes=[pltpu.VMEM((OUT_TT + V7X_SUBLANES, SC_WIDTH), F32),
                        pltpu.VMEM((OUT_TT + V7X_SUBLANES, LRU_WIDTH), F32),
                        pltpu.VMEM((OUT_TT, LRU_WIDTH), F32),
                        pltpu.VMEM((1, LRU_WIDTH), F32),
                        pltpu.VMEM((OUT_TT, LRU_WIDTH), BF16)],
        compiler_params=_params(("parallel", "arbitrary")),
        name="mix_out",
    )(x, xn, zb, w_lru_in, lcw, lcb, wa, ba, wx, bx, lam, w_in, cw, wlo, wro, wso, wout, gpost)


def _cast_kernel(w_ref, o_ref):
    o_ref[...] = w_ref[...].astype(BF16)


def _to_bf16(w, cols=None):
    depth, rows, width = w.shape
    lo, hi = cols or (0, width)
    assert lo % CAST_COLS == 0 and (hi - lo) % CAST_COLS == 0
    return pl.pallas_call(
        _cast_kernel,
        grid=(depth, (hi - lo) // CAST_COLS),
        in_specs=[pl.BlockSpec((None, rows, CAST_COLS), lambda l, j: (l, 0, lo // CAST_COLS + j))],
        out_specs=pl.BlockSpec((None, rows, CAST_COLS), lambda l, j: (l, 0, j)),
        out_shape=jax.ShapeDtypeStruct((depth, rows, hi - lo), BF16),
        compiler_params=_params(("parallel", "parallel")),
        name="to_bf16",
    )(w)


def _block_diag(w):
    per = V7X_MXU_DIM // LRU_BLOCK_DIM
    groups = LRU_BLOCKS // per
    w5 = w.reshape(DEPTH, groups, per, LRU_BLOCK_DIM, LRU_BLOCK_DIM)
    eye = jnp.eye(per, dtype=w.dtype)
    bd = jnp.einsum("lgaij,ab->lgaibj", w5, eye)
    return bd.reshape(DEPTH, groups, V7X_MXU_DIM, V7X_MXU_DIM).astype(BF16)


def kernel(x, positions, ffn1_pre_g, ffn1_w_in, ffn1_w_out, ffn1_post_g, mix_pre_g, w_mix_in, lru_conv_w, lru_conv_b, lru_w_a, lru_b_a, lru_w_x, lru_b_x, lru_lambda, w_lru_out, w_ret_out, sc_conv_w, w_sc_out, w_mix_out, mix_post_g, ffn2_pre_g, ffn2_w_in, ffn2_w_out, ffn2_post_g):
    vec = lambda v: v.reshape(DEPTH, 1, -1)
    cos, sin = _rope_tables(positions)
    wa_bd, wx_bd = _block_diag(lru_w_a), _block_diag(lru_w_x)
    w_lru_in = _to_bf16(w_mix_in, (0, LRU_IN))
    w_out_in = _to_bf16(w_mix_in, (LRU_IN + RET_IN, MIX_IN))
    wlo, wro, wso, wout = (_to_bf16(w) for w in (w_lru_out, w_ret_out, w_sc_out, w_mix_out))

    h = x.reshape(TOKENS, D_MODEL)
    for l in range(DEPTH):
        h, hn = _ffn(h, l, vec(ffn1_pre_g), ffn1_w_in, ffn1_w_out, vec(ffn1_post_g),
                     vec(mix_pre_g))
        zb = _retention(hn, l, w_mix_in, cos, sin)
        h = _mix_out(h, hn, zb, l, w_lru_in, lru_conv_w, vec(lru_conv_b), wa_bd, vec(lru_b_a),
                     wx_bd, vec(lru_b_x), vec(lru_lambda), w_out_in, sc_conv_w, wlo, wro, wso,
                     wout, vec(mix_post_g))
        h = _ffn(h, l, vec(ffn2_pre_g), ffn2_w_in, ffn2_w_out, vec(ffn2_post_g))
    return h.reshape(BATCH, SEQ, D_MODEL)
```
